```python
import jax, jax.numpy as jnp
from jax import lax
import numpy as np

D_MODEL = 1024
BATCH = 8
SEQ = 8192
DEPTH = 2
DEC_BATCH = 2
DEC_SEQ = 16384
PAST_LEN = 128

MIX_W = D_MODEL
ATT_W = MIX_W // 2
POOL_W = MIX_W - ATT_W
N_HEADS = 8
HEAD_NOPE = ATT_W // N_HEADS
HEAD_ROPE = HEAD_NOPE // 2
HEAD_V = ATT_W // N_HEADS
Q_LORA = D_MODEL // 4
KV_LORA = D_MODEL // 8
ROPE_THETA = 10000.0
Q_BLOCK = 128
POOL_WINDOWS = (2, 4, 8, 16)
N_POOL_GROUPS = 4
POOL_GW = POOL_W // N_POOL_GROUPS
IN_W = Q_LORA + KV_LORA + HEAD_ROPE + POOL_W
N_EXPERTS = 32
TOP_K = 4
D_FF = D_MODEL
SWIGLU_LIMIT = 7.0
SWIGLU_ALPHA = 1.702
MOE_BLOCK = 512
NORM_EPS = 1e-5

kernel_name = "hybrid_mla_pool_moe_encoder"

F32 = jnp.float32


def rms_norm(x, g):
    xf = x.astype(F32)
    y = xf * lax.rsqrt(jnp.mean(xf * xf, axis=-1, keepdims=True) + NORM_EPS)
    return (y * g.astype(F32)).astype(x.dtype)


def rope_tables(seq_len):
    inv = 1.0 / (ROPE_THETA ** (jnp.arange(0, HEAD_ROPE, 2, dtype=F32) / HEAD_ROPE))
    ang = jnp.arange(seq_len, dtype=F32)[:, None] * inv[None, :]
    return jnp.cos(ang), jnp.sin(ang)


def apply_rope(x, cos, sin):
    xf = x.astype(F32)
    x1, x2 = jnp.split(xf, 2, axis=-1)
    return jnp.concatenate([x1 * cos - x2 * sin, x1 * sin + x2 * cos], axis=-1).astype(x.dtype)


def mla_attention(q_nope, q_rope, k_nope, k_rope, v):
    B, S = q_nope.shape[:2]
    nq = S // Q_BLOCK
    scale = 1.0 / np.sqrt(HEAD_NOPE + HEAD_ROPE).astype(np.float32)
    qn = q_nope.reshape(B, nq, Q_BLOCK, N_HEADS, HEAD_NOPE).transpose(1, 0, 2, 3, 4)
    qr = q_rope.reshape(B, nq, Q_BLOCK, N_HEADS, HEAD_ROPE).transpose(1, 0, 2, 3, 4)

    def one_block(blk):
        qn_b, qr_b = blk
        s = (jnp.einsum('bqhd,bkhd->bhqk', qn_b, k_nope, preferred_element_type=F32)
             + jnp.einsum('bqhd,bkd->bhqk', qr_b, k_rope, preferred_element_type=F32))
        p = jax.nn.softmax(s * scale, axis=-1)
        return jnp.einsum('bhqk,bkhd->bqhd', p.astype(v.dtype), v)

    o = lax.map(one_block, (qn, qr))
    return o.transpose(1, 0, 2, 3, 4).reshape(B, S, N_HEADS * HEAD_V)


def pool_mixer(u, w_pool, pool_scale):
    B, S, _ = u.shape
    ug = u.reshape(B, S, N_POOL_GROUPS, POOL_GW).astype(F32)
    cs = jnp.concatenate([jnp.zeros((B, 1, N_POOL_GROUPS, POOL_GW), F32), jnp.cumsum(ug, axis=1)], axis=1)
    pos = jnp.arange(S)
    outs = []
    for g, w in enumerate(POOL_WINDOWS):
        lo = jnp.clip(pos - w // 2, 0, S)
        hi = jnp.clip(pos + w - w // 2, 0, S)
        cs_g = cs[:, :, g]
        mean = (cs_g[:, hi] - cs_g[:, lo]) / (hi - lo).astype(F32)[None, :, None]
        outs.append(mean - ug[:, :, g])
    pooled = jnp.stack(outs, axis=2).astype(u.dtype)
    y = jnp.einsum('bsgc,gcd->bsgd', pooled, w_pool) * pool_scale
    return y.reshape(B, S, POOL_W)


def moe(h, w_router, b_router, w_gate, b_gate, w_up, b_up, w_down, b_down):
    B, S, D = h.shape
    N = B * S
    NK = N * TOP_K
    xt = h.reshape(N, D)
    logits = jnp.matmul(xt, w_router, preferred_element_type=F32) + b_router.astype(F32)
    top_val, top_idx = lax.top_k(logits, TOP_K)
    gates = jax.nn.softmax(top_val, axis=-1)
    e_flat = top_idx.reshape(-1)
    g_flat = gates.reshape(-1)
    tok_flat = jnp.arange(NK, dtype=jnp.int32) // TOP_K
    order = jnp.argsort(e_flat)
    e_sorted = e_flat[order]
    counts = jnp.bincount(e_flat, length=N_EXPERTS)
    padded = (counts + MOE_BLOCK - 1) // MOE_BLOCK * MOE_BLOCK
    start = jnp.cumsum(counts) - counts
    pend = jnp.cumsum(padded)
    pstart = pend - padded
    dest = pstart[e_sorted] + jnp.arange(NK) - start[e_sorted]
    n_blocks = (NK + MOE_BLOCK - 1) // MOE_BLOCK + N_EXPERTS
    P = n_blocks * MOE_BLOCK
    slot_tok = jnp.full((P,), N, jnp.int32).at[dest].set(tok_flat[order])
    slot_gate = jnp.zeros((P,), F32).at[dest].set(g_flat[order])
    block_expert = jnp.minimum(
        jnp.searchsorted(pend, jnp.arange(n_blocks) * MOE_BLOCK, side='right'), N_EXPERTS - 1)
    x_pad = jnp.concatenate([xt, jnp.zeros((1, D), xt.dtype)], axis=0)

    def body(out, blk):
        tok, g, e = blk
        xb = x_pad[tok]
        gt = jnp.minimum(xb @ w_gate[e] + b_gate[e], SWIGLU_LIMIT)
        up = jnp.clip(xb @ w_up[e] + b_up[e], -SWIGLU_LIMIT, SWIGLU_LIMIT)
        act = gt * jax.nn.sigmoid(SWIGLU_ALPHA * gt) * (up + 1.0)
        y = (act @ w_down[e] + b_down[e]).astype(F32)
        return out.at[tok].add(y * g[:, None]), None

    out, _ = lax.scan(body, jnp.zeros((N + 1, D), F32),
                      (slot_tok.reshape(n_blocks, MOE_BLOCK), slot_gate.reshape(n_blocks, MOE_BLOCK), block_expert))
    return out[:N].reshape(B, S, D).astype(h.dtype)


def trunk(x, ln1, w_in, q_norm, w_uq, kv_norm, w_ukv, w_pool, pool_scale, attn_out_norm,
          pool_out_norm, w_out, ln2, w_router, b_router, w_gate, b_gate, w_up, b_up,
          w_down, b_down, final_norm):
    B, S, _ = x.shape
    cos, sin = rope_tables(S)
    splits = [Q_LORA, Q_LORA + KV_LORA, Q_LORA + KV_LORA + HEAD_ROPE]
    for l in range(DEPTH):
        h = rms_norm(x, ln1[l])
        proj = h @ w_in[l]
        c_q, c_kv, k_r, u = jnp.split(proj, splits, axis=-1)
        q = (rms_norm(c_q, q_norm[l]) @ w_uq[l]).reshape(B, S, N_HEADS, HEAD_NOPE + HEAD_ROPE)
        q_nope = q[..., :HEAD_NOPE]
        q_rope = apply_rope(q[..., HEAD_NOPE:], cos[:, None, :], sin[:, None, :])
        kv = (rms_norm(c_kv, kv_norm[l]) @ w_ukv[l]).reshape(B, S, N_HEADS, HEAD_NOPE + HEAD_V)
        k_nope, v = kv[..., :HEAD_NOPE], kv[..., HEAD_NOPE:]
        k_rope = apply_rope(k_r, cos, sin)
        attn = mla_attention(q_nope, q_rope, k_nope, k_rope, v)
        pool = pool_mixer(u, w_pool[l], pool_scale[l])
        mixed = jnp.concatenate([rms_norm(attn, attn_out_norm[l]), rms_norm(pool, pool_out_norm[l])], axis=-1)
        x = x + mixed @ w_out[l]
        h = rms_norm(x, ln2[l])
        x = x + moe(h, w_router[l], b_router[l], w_gate[l], b_gate[l], w_up[l], b_up[l], w_down[l], b_down[l])
    return rms_norm(x, final_norm)


def setup_inputs(seed: int = 0) -> dict:
    key = jax.random.key(seed)
    ks = jax.random.split(key, 32)
    L, E = DEPTH, N_EXPERTS

    def nrm(k, shape, scale):
        return jax.random.normal(k, shape, F32) * scale

    def gain(k, shape):
        return 1.0 + 0.05 * jax.random.normal(k, shape, F32)

    return {
        "x_prompt": jax.random.normal(ks[0], (BATCH, SEQ, D_MODEL), F32),
        "x_sample": jax.random.normal(ks[1], (DEC_BATCH, DEC_SEQ, D_MODEL), F32),
        "ln1": gain(ks[2], (L, D_MODEL)),
        "w_in": nrm(ks[3], (L, D_MODEL, IN_W), D_MODEL ** -0.5),
        "q_norm": gain(ks[4], (L, Q_LORA)),
        "w_uq": nrm(ks[5], (L, Q_LORA, N_HEADS * (HEAD_NOPE + HEAD_ROPE)), Q_LORA ** -0.5),
        "kv_norm": gain(ks[6], (L, KV_LORA)),
        "w_ukv": nrm(ks[7], (L, KV_LORA, N_HEADS * (HEAD_NOPE + HEAD_V)), KV_LORA ** -0.5),
        "w_pool": nrm(ks[8], (L, N_POOL_GROUPS, POOL_GW, POOL_GW), POOL_GW ** -0.5),
        "pool_scale": 1.0 + 0.1 * jax.random.normal(ks[9], (L, N_POOL_GROUPS, POOL_GW), F32),
        "attn_out_norm": gain(ks[10], (L, ATT_W)),
        "pool_out_norm": gain(ks[11], (L, POOL_W)),
        "w_out": nrm(ks[12], (L, MIX_W, D_MODEL), 0.5 * MIX_W ** -0.5),
        "ln2": gain(ks[13], (L, D_MODEL)),
        "w_router": nrm(ks[14], (L, D_MODEL, E), D_MODEL ** -0.5),
        "b_router": nrm(ks[15], (L, E), 0.01),
        "w_gate": nrm(ks[16], (L, E, D_MODEL, D_FF), D_MODEL ** -0.5),
        "b_gate": nrm(ks[17], (L, E, D_FF), 0.01),
        "w_up": nrm(ks[18], (L, E, D_MODEL, D_FF), D_MODEL ** -0.5),
        "b_up": nrm(ks[19], (L, E, D_FF), 0.01),
        "w_down": nrm(ks[20], (L, E, D_FF, D_MODEL), 0.5 * D_FF ** -0.5),
        "b_down": nrm(ks[21], (L, E, D_MODEL), 0.01),
        "final_norm": gain(ks[22], (D_MODEL,)),
    }


def reference(x_prompt, x_sample, ln1, w_in, q_norm, w_uq, kv_norm, w_ukv, w_pool, pool_scale,
              attn_out_norm, pool_out_norm, w_out, ln2, w_router, b_router, w_gate, b_gate,
              w_up, b_up, w_down, b_down, final_norm):
    y_prompt = trunk(x_prompt, ln1, w_in, q_norm, w_uq, kv_norm, w_ukv, w_pool, pool_scale,
                     attn_out_norm, pool_out_norm, w_out, ln2, w_router, b_router, w_gate, b_gate,
                     w_up, b_up, w_down, b_down, final_norm)
    y_sample = trunk(x_sample, ln1, w_in, q_norm, w_uq, kv_norm, w_ukv, w_pool, pool_scale,
                     attn_out_norm, pool_out_norm, w_out, ln2, w_router, b_router, w_gate, b_gate,
                     w_up, b_up, w_down, b_down, final_norm)
    return (y_prompt, y_sample)
```

```python
import functools

import jax
import jax.numpy as jnp
import numpy as np
from jax import lax
from jax.experimental import pallas as pl
from jax.experimental.pallas import tpu as pltpu

F32 = jnp.float32
BF16 = jnp.bfloat16
I32 = jnp.int32

D_MODEL = 1024
N_HEADS = 8
HEAD_NOPE = 64
HEAD_ROPE = 32
HEAD_V = 64
HEAD_PAD = 128
Q_LORA = 256
KV_LORA = 128
ATT_W = 512
POOL_W = 512
POOL_WINDOWS = (2, 4, 8, 16)
POOL_GW = 128
POOL_HALO = 8
ROPE_THETA = 10000.0
N_EXPERTS = 32
TOP_K = 4
SWIGLU_LIMIT = 7.0
SWIGLU_ALPHA = 1.702
NORM_EPS = 1e-5
ROUTER_PAD = 128

_C_Q = 0
_C_KV = _C_Q + Q_LORA
_C_KRA = _C_KV + KV_LORA
_C_KRB = _C_KRA + HEAD_PAD
_C_U = _C_KRB + HEAD_PAD
IN_W_EXT = _C_U + POOL_W

VMEM_LIMIT = 56 * 1024 * 1024


def _cparams(n_axes):
    return pltpu.CompilerParams(dimension_semantics=("arbitrary",) * n_axes,
                                vmem_limit_bytes=VMEM_LIMIT)


def _rms(x, g):
    return x * lax.rsqrt(jnp.mean(x * x, axis=-1, keepdims=True) + NORM_EPS) * g


def _proj_kernel(x_ref, ln1_ref, win_ref, qn_ref, wqa_ref, wqb_ref, kvn_ref, wk_ref, wv_ref,
                 cq_ref, sq_ref, ck_ref, sk_ref, q_ref, k_ref, v_ref, u_ref):
    h = _rms(x_ref[...], ln1_ref[...]).astype(BF16)
    proj = jnp.dot(h, win_ref[...], preferred_element_type=F32)
    u_ref[...] = proj[:, _C_U:_C_U + POOL_W]
    hq = _rms(proj[:, _C_Q:_C_Q + Q_LORA], qn_ref[...]).astype(BF16)
    qa = jnp.dot(hq, wqa_ref[...], preferred_element_type=F32)
    qb = jnp.dot(hq, wqb_ref[...], preferred_element_type=F32)
    hkv = _rms(proj[:, _C_KV:_C_KV + KV_LORA], kvn_ref[...]).astype(BF16)
    kn = jnp.dot(hkv, wk_ref[...], preferred_element_type=F32)
    v_ref[...] = jnp.dot(hkv, wv_ref[...], preferred_element_type=F32).astype(BF16)
    k_rope = (proj[:, _C_KRA:_C_KRA + HEAD_PAD] * ck_ref[...]
              + proj[:, _C_KRB:_C_KRB + HEAD_PAD] * sk_ref[...])
    cq = cq_ref[...]
    sq = sq_ref[...]
    for hd in range(N_HEADS):
        sl = slice(hd * HEAD_PAD, (hd + 1) * HEAD_PAD)
        q_ref[:, sl] = (qa[:, sl] * cq + qb[:, sl] * sq).astype(BF16)
        k_ref[:, sl] = (kn[:, sl] + k_rope).astype(BF16)


def _proj_call(x, lw, tabs, seq_len, tm):
    T = x.shape[0]
    tiles_per_seq = seq_len // tm
    full = lambda shape: pl.BlockSpec(shape, lambda i: (0,) * len(shape))
    tab = pl.BlockSpec((tm, HEAD_PAD), lambda i: (i % tiles_per_seq, 0))
    row = lambda w: pl.BlockSpec((tm, w), lambda i: (i, 0))
    return pl.pallas_call(
        _proj_kernel,
        grid=(T // tm,),
        in_specs=[row(D_MODEL), full((1, D_MODEL)), full((D_MODEL, IN_W_EXT)),
                  full((1, Q_LORA)), full((Q_LORA, N_HEADS * HEAD_PAD)),
                  full((Q_LORA, N_HEADS * HEAD_PAD)),
                  full((1, KV_LORA)), full((KV_LORA, N_HEADS * HEAD_PAD)),
                  full((KV_LORA, ATT_W)), tab, tab, tab, tab],
        out_specs=[row(N_HEADS * HEAD_PAD), row(N_HEADS * HEAD_PAD), row(ATT_W), row(POOL_W)],
        out_shape=[jax.ShapeDtypeStruct((T, N_HEADS * HEAD_PAD), BF16),
                   jax.ShapeDtypeStruct((T, N_HEADS * HEAD_PAD), BF16),
                   jax.ShapeDtypeStruct((T, ATT_W), BF16),
                   jax.ShapeDtypeStruct((T, POOL_W), F32)],
        compiler_params=_cparams(1),
        name="proj",
    )(x, lw["ln1"], lw["w_in"], lw["q_norm"], lw["w_qa"], lw["w_qb"], lw["kv_norm"],
      lw["w_k"], lw["w_v"], tabs["cq"], tabs["sq"], tabs["ck"], tabs["sk"])


def _attn_kernel(q_ref, k_ref, v_ref, o_ref, *, tk, n_chunks):
    outs = []
    for hh in range(2):
        cols = slice(hh * HEAD_PAD, (hh + 1) * HEAD_PAD)
        q = q_ref[:, cols]

        def scores(j):
            kc = k_ref[pl.ds(pl.multiple_of(j * tk, tk), tk), cols]
            return lax.dot_general(q, kc, (((1,), (1,)), ((), ())),
                                   preferred_element_type=F32)

        def values(j):
            return v_ref[pl.ds(pl.multiple_of(j * tk, tk), tk), :]

        s = scores(0)
        m = jnp.max(s, axis=-1, keepdims=True)
        p = jnp.exp(s - m)
        l = jnp.sum(p, axis=-1, keepdims=True)
        acc = jnp.dot(p.astype(BF16), values(0), preferred_element_type=F32)

        def step(j, carry):
            m, l, acc = carry
            s = scores(j)
            m_new = jnp.maximum(m, jnp.max(s, axis=-1, keepdims=True))
            alpha = jnp.exp(m - m_new)
            p = jnp.exp(s - m_new)
            l = alpha * l + jnp.sum(p, axis=-1, keepdims=True)
            acc = alpha * acc + jnp.dot(p.astype(BF16), values(j), preferred_element_type=F32)
            return m_new, l, acc

        m, l, acc = lax.fori_loop(1, n_chunks, step, (m, l, acc))
        outs.append(acc / l)
    lane = lax.broadcasted_iota(I32, outs[0].shape, 1)
    o_ref[...] = jnp.where(lane < HEAD_V, outs[0], outs[1]).astype(o_ref.dtype)


def _attn_call(q, k, v, batch, seq_len, tq, tk):
    T = q.shape[0]
    n_q = seq_len // tq
    kern = functools.partial(_attn_kernel, tk=tk, n_chunks=seq_len // tk)
    return pl.pallas_call(
        kern,
        grid=(batch, N_HEADS // 2, n_q),
        in_specs=[pl.BlockSpec((tq, 2 * HEAD_PAD), lambda b, hp, i: (b * n_q + i, hp)),
                  pl.BlockSpec((seq_len, 2 * HEAD_PAD), lambda b, hp, i: (b, hp)),
                  pl.BlockSpec((seq_len, 2 * HEAD_V), lambda b, hp, i: (b, hp))],
        out_specs=pl.BlockSpec((tq, 2 * HEAD_V), lambda b, hp, i: (b * n_q + i, hp)),
        out_shape=jax.ShapeDtypeStruct((T, ATT_W), BF16),
        compiler_params=_cparams(3),
        name="attn",
    )(q, k, v)


def _mix_kernel(x_ref, o_ref, u_ref, up_ref, un_ref, an_ref, pn_ref, wpool_ref, pscale_ref,
                wout_ref, ln2_ref, wrh_ref, wrl_ref, br_ref,
                xn_ref, h2_ref, idx_ref, gate_ref, ubuf, *, tm, seq_len):
    i = pl.program_id(0)
    tiles_per_seq = seq_len // tm
    ti = i % tiles_per_seq
    prev_ok = jnp.where(ti > 0, 1.0, 0.0).astype(F32)
    next_ok = jnp.where(ti < tiles_per_seq - 1, 1.0, 0.0).astype(F32)
    ubuf[0:POOL_HALO, :] = up_ref[...] * prev_ok
    ubuf[POOL_HALO:POOL_HALO + tm, :] = u_ref[...]
    ubuf[POOL_HALO + tm:2 * POOL_HALO + tm, :] = un_ref[...] * next_ok

    pos = ti * tm + lax.broadcasted_iota(I32, (tm, 1), 0)
    pooled = []
    for g, w in enumerate(POOL_WINDOWS):
        cols = slice(g * POOL_GW, (g + 1) * POOL_GW)
        tot = jnp.zeros((tm, POOL_GW), F32)
        for d in range(-(w // 2), w - w // 2):
            tot = tot + ubuf[POOL_HALO + d:POOL_HALO + d + tm, cols]
        lo = jnp.maximum(pos - w // 2, 0)
        hi = jnp.minimum(pos + (w - w // 2), seq_len)
        cnt = (hi - lo).astype(F32)
        pg = (tot / cnt - u_ref[:, cols]).astype(BF16)
        yg = jnp.dot(pg, wpool_ref[g], preferred_element_type=F32) * pscale_ref[:, cols]
        pooled.append(yg)
    pool = jnp.concatenate(pooled, axis=-1)
    attn = o_ref[...].astype(F32)
    mixed = jnp.concatenate([_rms(attn, an_ref[...]), _rms(pool, pn_ref[...])],
                            axis=-1).astype(BF16)
    xn = x_ref[...] + jnp.dot(mixed, wout_ref[...], preferred_element_type=F32)
    xn_ref[...] = xn
    h2 = _rms(xn, ln2_ref[...])
    h2_ref[...] = h2
    hi_ = h2.astype(BF16)
    lo_ = (h2 - hi_.astype(F32)).astype(BF16)
    logits = (jnp.dot(hi_, wrh_ref[...], preferred_element_type=F32)
              + jnp.dot(lo_, wrh_ref[...], preferred_element_type=F32)
              + jnp.dot(hi_, wrl_ref[...], preferred_element_type=F32)) + br_ref[...]
    lane = lax.broadcasted_iota(I32, logits.shape, 1)
    vals, idxs = [], []
    for _ in range(TOP_K):
        mx = jnp.max(logits, axis=-1, keepdims=True)
        ix = jnp.min(jnp.where(logits == mx, lane, ROUTER_PAD), axis=-1, keepdims=True)
        vals.append(mx)
        idxs.append(ix)
        logits = jnp.where(lane == ix, -jnp.inf, logits)
    es = [jnp.exp(vk - vals[0]) for vk in vals]
    den = es[0] + es[1] + es[2] + es[3]
    idx_out = jnp.zeros(lane.shape, I32)
    gate_out = jnp.zeros(lane.shape, F32)
    for kk in range(TOP_K):
        idx_out = jnp.where(lane == kk, idxs[kk], idx_out)
        gate_out = jnp.where(lane == kk, es[kk] / den, gate_out)
    idx_ref[...] = idx_out
    gate_ref[...] = gate_out


def _mix_call(x, o, u, lw, seq_len, tm):
    T = x.shape[0]
    hb = tm // POOL_HALO
    n_hblk = T // POOL_HALO
    full = lambda shape: pl.BlockSpec(shape, lambda i: (0,) * len(shape))
    row = lambda w: pl.BlockSpec((tm, w), lambda i: (i, 0))
    kern = functools.partial(_mix_kernel, tm=tm, seq_len=seq_len)
    return pl.pallas_call(
        kern,
        grid=(T // tm,),
        in_specs=[row(D_MODEL), row(ATT_W), row(POOL_W),
                  pl.BlockSpec((POOL_HALO, POOL_W), lambda i: (jnp.maximum(i * hb - 1, 0), 0)),
                  pl.BlockSpec((POOL_HALO, POOL_W),
                               lambda i: (jnp.minimum((i + 1) * hb, n_hblk - 1), 0)),
                  full((1, ATT_W)), full((1, POOL_W)), full((4, POOL_GW, POOL_GW)),
                  full((1, POOL_W)), full((ATT_W + POOL_W, D_MODEL)), full((1, D_MODEL)),
                  full((D_MODEL, ROUTER_PAD)), full((D_MODEL, ROUTER_PAD)),
                  full((1, ROUTER_PAD))],
        out_specs=[row(D_MODEL), row(D_MODEL), row(ROUTER_PAD), row(ROUTER_PAD)],
        out_shape=[jax.ShapeDtypeStruct((T, D_MODEL), F32),
                   jax.ShapeDtypeStruct((T, D_MODEL), F32),
                   jax.ShapeDtypeStruct((T, ROUTER_PAD), I32),
                   jax.ShapeDtypeStruct((T, ROUTER_PAD), F32)],
        scratch_shapes=[pltpu.VMEM((tm + 2 * POOL_HALO, POOL_W), F32)],
        compiler_params=_cparams(1),
        name="mix",
    )(x, o, u, u, u, lw["attn_out_norm"], lw["pool_out_norm"], lw["w_pool"], lw["pool_scale"],
      lw["w_out"], lw["ln2"], lw["w_r_hi"], lw["w_r_lo"], lw["b_r"])


def _dispatch_kernel(dest_ref, h_ref, xs_in_ref, xs_ref, sem, *, tm):
    del xs_in_ref

    def issue(r, c):
        for kk in range(TOP_K):
            d = dest_ref[r * TOP_K + kk]
            pltpu.make_async_copy(h_ref.at[pl.ds(r, 1)], xs_ref.at[pl.ds(d, 1)], sem).start()
        return c

    lax.fori_loop(0, tm, issue, 0)
    for _ in range(TOP_K):
        pltpu.make_async_copy(h_ref, xs_ref.at[pl.ds(0, tm)], sem).wait()


def _dispatch_call(h2, dest_flat, xs_init, tm):
    T = h2.shape[0]
    kern = functools.partial(_dispatch_kernel, tm=tm)
    return pl.pallas_call(
        kern,
        grid=(T // tm,),
        in_specs=[pl.BlockSpec((tm * TOP_K,), lambda i: (i,), memory_space=pltpu.SMEM),
                  pl.BlockSpec((tm, D_MODEL), lambda i: (i, 0)),
                  pl.BlockSpec(memory_space=pl.ANY)],
        out_specs=pl.BlockSpec(memory_space=pl.ANY),
        out_shape=jax.ShapeDtypeStruct(xs_init.shape, xs_init.dtype),
        scratch_shapes=[pltpu.SemaphoreType.DMA],
        input_output_aliases={2: 0},
        compiler_params=_cparams(1),
        name="dispatch",
    )(dest_flat, h2, xs_init)


def _ffn_kernel(be_ref, nu_ref, xs_ref, wg_ref, bg_ref, wu_ref, bu_ref, wd_ref, bd_ref, y_ref):
    del be_ref

    @pl.when(pl.program_id(0) < nu_ref[0])
    def _():
        x = xs_ref[...].astype(BF16)
        gt = jnp.minimum(jnp.dot(x, wg_ref[0], preferred_element_type=F32) + bg_ref[0],
                         SWIGLU_LIMIT)
        up = jnp.clip(jnp.dot(x, wu_ref[0], preferred_element_type=F32) + bu_ref[0],
                      -SWIGLU_LIMIT, SWIGLU_LIMIT)
        act = gt * jax.nn.sigmoid(SWIGLU_ALPHA * gt) * (up + 1.0)
        y_ref[...] = jnp.dot(act.astype(BF16), wd_ref[0], preferred_element_type=F32) + bd_ref[0]

    @pl.when(pl.program_id(0) >= nu_ref[0])
    def _():
        y_ref[...] = jnp.zeros(y_ref.shape, y_ref.dtype)


def _ffn_call(xs, block_expert, n_used, lw, bm):
    P = xs.shape[0]
    n_blocks = P // bm
    xmap = lambda i, be, nu: (jnp.minimum(i, nu[0] - 1), 0)
    ymap = lambda i, be, nu: (i, 0)
    wmap = lambda i, be, nu: (be[i], 0, 0)
    grid_spec = pltpu.PrefetchScalarGridSpec(
        num_scalar_prefetch=2,
        grid=(n_blocks,),
        in_specs=[pl.BlockSpec((bm, D_MODEL), xmap),
                  pl.BlockSpec((1, D_MODEL, D_MODEL), wmap), pl.BlockSpec((1, 1, D_MODEL), wmap),
                  pl.BlockSpec((1, D_MODEL, D_MODEL), wmap), pl.BlockSpec((1, 1, D_MODEL), wmap),
                  pl.BlockSpec((1, D_MODEL, D_MODEL), wmap), pl.BlockSpec((1, 1, D_MODEL), wmap)],
        out_specs=pl.BlockSpec((bm, D_MODEL), ymap),
    )
    return pl.pallas_call(
        _ffn_kernel,
        grid_spec=grid_spec,
        out_shape=jax.ShapeDtypeStruct((P, D_MODEL), F32),
        compiler_params=_cparams(1),
        name="ffn",
    )(block_expert, n_used, xs, lw["w_gate"], lw["b_gate"], lw["w_up"], lw["b_up"],
      lw["w_down"], lw["b_down"])


def _combine_kernel(dest_ref, x_ref, gate_ref, fn_ref, y_ref, out_ref, ybuf, sem, *, tm, final):
    def issue(r, c):
        for kk in range(TOP_K):
            d = dest_ref[r * TOP_K + kk]
            pltpu.make_async_copy(y_ref.at[pl.ds(d, 1)], ybuf.at[kk, pl.ds(r, 1)], sem).start()
        return c

    lax.fori_loop(0, tm, issue, 0)
    for kk in range(TOP_K):
        pltpu.make_async_copy(y_ref.at[pl.ds(0, tm)], ybuf.at[kk], sem).wait()
    g = gate_ref[...]
    acc = x_ref[...]
    for kk in range(TOP_K):
        acc = acc + ybuf[kk] * g[:, kk:kk + 1]
    if final:
        acc = _rms(acc, fn_ref[...])
    out_ref[...] = acc


def _combine_call(x, gates, dest_flat, y, final_norm, tm, final):
    T = x.shape[0]
    kern = functools.partial(_combine_kernel, tm=tm, final=final)
    return pl.pallas_call(
        kern,
        grid=(T // tm,),
        in_specs=[pl.BlockSpec((tm * TOP_K,), lambda i: (i,), memory_space=pltpu.SMEM),
                  pl.BlockSpec((tm, D_MODEL), lambda i: (i, 0)),
                  pl.BlockSpec((tm, ROUTER_PAD), lambda i: (i, 0)),
                  pl.BlockSpec((1, D_MODEL), lambda i: (0, 0)),
                  pl.BlockSpec(memory_space=pl.ANY)],
        out_specs=pl.BlockSpec((tm, D_MODEL), lambda i: (i, 0)),
        out_shape=jax.ShapeDtypeStruct((T, D_MODEL), F32),
        scratch_shapes=[pltpu.VMEM((TOP_K, tm, D_MODEL), F32), pltpu.SemaphoreType.DMA],
        compiler_params=_cparams(1),
        name="combine",
    )(dest_flat, x, gates, final_norm, y)


def _routing_tables(top_idx, bm):
    T = top_idx.shape[0]
    onehot = (top_idx[:, :, None] == jnp.arange(N_EXPERTS, dtype=I32)[None, None, :])
    per_tok = jnp.sum(onehot.astype(I32), axis=1)
    incl = jnp.cumsum(per_tok, axis=0)
    rank = jnp.take_along_axis(incl - per_tok, top_idx, axis=1)
    counts = incl[-1]
    padded = (counts + bm - 1) // bm * bm
    pend = jnp.cumsum(padded)
    pstart = pend - padded
    dest = pstart[top_idx] + rank
    n_blocks = (T * TOP_K + bm - 1) // bm + N_EXPERTS
    block_expert = jnp.minimum(
        jnp.searchsorted(pend, jnp.arange(n_blocks, dtype=I32) * bm, side="right"),
        N_EXPERTS - 1).astype(I32)
    n_used = (pend[-1] // bm).astype(I32).reshape(1)
    return dest.reshape(-1).astype(I32), block_expert, n_used, n_blocks


def _rope_tables(seq_len):
    inv = 1.0 / (ROPE_THETA ** (jnp.arange(0, HEAD_ROPE, 2, dtype=F32) / HEAD_ROPE))
    ang = jnp.arange(seq_len, dtype=F32)[:, None] * inv[None, :]
    cos, sin = jnp.cos(ang), jnp.sin(ang)
    scale = 1.0 / np.sqrt(HEAD_NOPE + HEAD_ROPE).astype(np.float32)
    z = jnp.zeros((seq_len, HEAD_PAD - HEAD_NOPE - HEAD_ROPE), F32)
    ones = jnp.ones((seq_len, HEAD_NOPE), F32)
    zeros = jnp.zeros((seq_len, HEAD_NOPE), F32)
    ck = jnp.concatenate([zeros, cos, cos, z], axis=1)
    sk = jnp.concatenate([zeros, sin, sin, z], axis=1)
    cq = jnp.concatenate([ones, cos, cos, z], axis=1) * scale
    sq = sk * scale
    return {"cq": cq, "sq": sq, "ck": ck, "sk": sk}


def _pad_heads(w_nope, w_r1, w_r2):
    lead = w_nope.shape[:-2]
    z = jnp.zeros(lead + (N_HEADS, HEAD_PAD - HEAD_NOPE - HEAD_ROPE), w_nope.dtype)
    return jnp.concatenate([w_nope, w_r1, w_r2, z], axis=-1).reshape(lead + (N_HEADS * HEAD_PAD,))


def _prep_layer(l, ln1, w_in, q_norm, w_uq, kv_norm, w_ukv, w_pool, pool_scale, attn_out_norm,
                pool_out_norm, w_out, ln2, w_router, b_router, w_gate, b_gate, w_up, b_up,
                w_down, b_down):
    half = HEAD_ROPE // 2
    wi = w_in[l]
    kr = wi[:, Q_LORA + KV_LORA:Q_LORA + KV_LORA + HEAD_ROPE]
    kr1, kr2 = kr[:, :half], kr[:, half:]
    zn = jnp.zeros((D_MODEL, HEAD_NOPE), F32)
    zp = jnp.zeros((D_MODEL, HEAD_PAD - HEAD_NOPE - HEAD_ROPE), F32)
    kra = jnp.concatenate([zn, kr1, kr2, zp], axis=1)
    krb = jnp.concatenate([zn, -kr2, kr1, zp], axis=1)
    w_in_ext = jnp.concatenate(
        [wi[:, :Q_LORA + KV_LORA], kra, krb, wi[:, Q_LORA + KV_LORA + HEAD_ROPE:]], axis=1)
    wq = w_uq[l].reshape(Q_LORA, N_HEADS, HEAD_NOPE + HEAD_ROPE)
    qn_, q1, q2 = wq[..., :HEAD_NOPE], wq[..., HEAD_NOPE:HEAD_NOPE + half], wq[..., HEAD_NOPE + half:]
    w_qa = _pad_heads(qn_, q1, q2)
    w_qb = _pad_heads(jnp.zeros_like(qn_), -q2, q1)
    wkv = w_ukv[l].reshape(KV_LORA, N_HEADS, HEAD_NOPE + HEAD_V)
    kz = jnp.zeros((KV_LORA, N_HEADS, half), F32)
    w_k = _pad_heads(wkv[..., :HEAD_NOPE], kz, kz)
    w_v = wkv[..., HEAD_NOPE:].reshape(KV_LORA, N_HEADS * HEAD_V)
    wr = jnp.pad(w_router[l], ((0, 0), (0, ROUTER_PAD - N_EXPERTS)))
    wr_hi = wr.astype(BF16)
    wr_lo = (wr - wr_hi.astype(F32)).astype(BF16)
    br = jnp.concatenate([b_router[l].astype(F32),
                          jnp.full((ROUTER_PAD - N_EXPERTS,), -jnp.inf, F32)]).reshape(1, ROUTER_PAD)
    return {
        "ln1": ln1[l].reshape(1, -1), "w_in": w_in_ext.astype(BF16),
        "q_norm": q_norm[l].reshape(1, -1), "w_qa": w_qa.astype(BF16), "w_qb": w_qb.astype(BF16),
        "kv_norm": kv_norm[l].reshape(1, -1), "w_k": w_k.astype(BF16), "w_v": w_v.astype(BF16),
        "w_pool": w_pool[l].astype(BF16), "pool_scale": pool_scale[l].reshape(1, -1),
        "attn_out_norm": attn_out_norm[l].reshape(1, -1),
        "pool_out_norm": pool_out_norm[l].reshape(1, -1),
        "w_out": w_out[l].astype(BF16), "ln2": ln2[l].reshape(1, -1),
        "w_r_hi": wr_hi, "w_r_lo": wr_lo, "b_r": br,
        "w_gate": w_gate[l].astype(BF16), "b_gate": b_gate[l].reshape(N_EXPERTS, 1, -1),
        "w_up": w_up[l].astype(BF16), "b_up": b_up[l].reshape(N_EXPERTS, 1, -1),
        "w_down": w_down[l].astype(BF16), "b_down": b_down[l].reshape(N_EXPERTS, 1, -1),
    }


def _tiles(seq_len):
    tm = min(512, seq_len)
    tq = min(256, seq_len)
    tk = min(512, seq_len)
    tc = min(256, seq_len)
    bm = 512
    return tm, tq, tk, tc, bm


def _trunk(x3, layers, final_norm):
    B, S, _ = x3.shape
    T = B * S
    tm, tq, tk, tc, bm = _tiles(S)
    tabs = _rope_tables(S)
    x = x3.reshape(T, D_MODEL)
    fn = final_norm.reshape(1, -1)
    for li, lw in enumerate(layers):
        q, k, v, u = _proj_call(x, lw, tabs, S, tm)
        o = _attn_call(q, k, v, B, S, tq, tk)
        x, h2, idx_pad, gate_pad = _mix_call(x, o, u, lw, S, tm)
        dest, block_expert, n_used, n_blocks = _routing_tables(idx_pad[:, :TOP_K], bm)
        xs = _dispatch_call(h2, dest, jnp.zeros((n_blocks * bm, D_MODEL), F32), tc)
        y = _ffn_call(xs, block_expert, n_used, lw, bm)
        x = _combine_call(x, gate_pad, dest, y, fn, tc, final=(li == len(layers) - 1))
    return x.reshape(B, S, D_MODEL)


def kernel(x_prompt, x_sample, ln1, w_in, q_norm, w_uq, kv_norm, w_ukv, w_pool, pool_scale,
           attn_out_norm, pool_out_norm, w_out, ln2, w_router, b_router, w_gate, b_gate, w_up,
           b_up, w_down, b_down, final_norm):
    params = (ln1, w_in, q_norm, w_uq, kv_norm, w_ukv, w_pool, pool_scale, attn_out_norm,
              pool_out_norm, w_out, ln2, w_router, b_router, w_gate, b_gate, w_up, b_up,
              w_down, b_down)
    layers = [_prep_layer(l, *params) for l in range(ln1.shape[0])]
    return (_trunk(x_prompt, layers, final_norm), _trunk(x_sample, layers, final_norm))
```

```python
import functools

import jax
import jax.numpy as jnp
import numpy as np
from jax import lax
from jax.experimental import pallas as pl
from jax.experimental.pallas import tpu as pltpu

F32 = jnp.float32
BF16 = jnp.bfloat16
I32 = jnp.int32

D_MODEL = 1024
N_HEADS = 8
HEAD_NOPE = 64
HEAD_ROPE = 32
HEAD_V = 64
HEAD_PAD = 128
Q_LORA = 256
KV_LORA = 128
ATT_W = 512
POOL_W = 512
POOL_WINDOWS = (2, 4, 8, 16)
POOL_GW = 128
POOL_HALO = 8
ROPE_THETA = 10000.0
N_EXPERTS = 32
TOP_K = 4
SWIGLU_LIMIT = 7.0
SWIGLU_ALPHA = 1.702
NORM_EPS = 1e-5
ROUTER_PAD = 128

_C_Q = 0
_C_KV = _C_Q + Q_LORA
_C_KRA = _C_KV + KV_LORA
_C_KRB = _C_KRA + HEAD_PAD
_C_U = _C_KRB + HEAD_PAD
IN_W_EXT = _C_U + POOL_W

VMEM_LIMIT = 56 * 1024 * 1024


def _cparams(n_axes):
    return pltpu.CompilerParams(dimension_semantics=("arbitrary",) * n_axes,
                                vmem_limit_bytes=VMEM_LIMIT)


def _rms(x, g):
    return x * lax.rsqrt(jnp.mean(x * x, axis=-1, keepdims=True) + NORM_EPS) * g


def _proj_kernel(x_ref, ln1_ref, win_ref, qn_ref, wqa_ref, wqb_ref, kvn_ref, wk_ref, wv_ref,
                 cq_ref, sq_ref, ck_ref, sk_ref, q_ref, k_ref, v_ref, u_ref):
    h = _rms(x_ref[...], ln1_ref[...]).astype(BF16)
    proj = jnp.dot(h, win_ref[...], preferred_element_type=F32)
    u_ref[...] = proj[:, _C_U:_C_U + POOL_W]
    hq = _rms(proj[:, _C_Q:_C_Q + Q_LORA], qn_ref[...]).astype(BF16)
    qa = jnp.dot(hq, wqa_ref[...], preferred_element_type=F32)
    qb = jnp.dot(hq, wqb_ref[...], preferred_element_type=F32)
    hkv = _rms(proj[:, _C_KV:_C_KV + KV_LORA], kvn_ref[...]).astype(BF16)
    kn = jnp.dot(hkv, wk_ref[...], preferred_element_type=F32)
    vv = jnp.dot(hkv, wv_ref[...], preferred_element_type=F32)
    lane = lax.broadcasted_iota(I32, vv.shape, 1)
    v_ref[...] = jnp.where((lane & (HEAD_PAD - 1)) == HEAD_V, 1.0, vv).astype(BF16)
    k_rope = (proj[:, _C_KRA:_C_KRA + HEAD_PAD] * ck_ref[...]
              + proj[:, _C_KRB:_C_KRB + HEAD_PAD] * sk_ref[...])
    cq = cq_ref[...]
    sq = sq_ref[...]
    for hd in range(N_HEADS):
        sl = slice(hd * HEAD_PAD, (hd + 1) * HEAD_PAD)
        q_ref[:, sl] = (qa[:, sl] * cq + qb[:, sl] * sq).astype(BF16)
        k_ref[:, sl] = (kn[:, sl] + k_rope).astype(BF16)


def _proj_call(x, lw, tabs, seq_len, tm):
    T = x.shape[0]
    tiles_per_seq = seq_len // tm
    full = lambda shape: pl.BlockSpec(shape, lambda i: (0,) * len(shape))
    tab = pl.BlockSpec((tm, HEAD_PAD), lambda i: (i % tiles_per_seq, 0))
    row = lambda w: pl.BlockSpec((tm, w), lambda i: (i, 0))
    return pl.pallas_call(
        _proj_kernel,
        grid=(T // tm,),
        in_specs=[row(D_MODEL), full((1, D_MODEL)), full((D_MODEL, IN_W_EXT)),
                  full((1, Q_LORA)), full((Q_LORA, N_HEADS * HEAD_PAD)),
                  full((Q_LORA, N_HEADS * HEAD_PAD)),
                  full((1, KV_LORA)), full((KV_LORA, N_HEADS * HEAD_PAD)),
                  full((KV_LORA, N_HEADS * HEAD_PAD)), tab, tab, tab, tab],
        out_specs=[row(N_HEADS * HEAD_PAD), row(N_HEADS * HEAD_PAD), row(N_HEADS * HEAD_PAD),
                   row(POOL_W)],
        out_shape=[jax.ShapeDtypeStruct((T, N_HEADS * HEAD_PAD), BF16),
                   jax.ShapeDtypeStruct((T, N_HEADS * HEAD_PAD), BF16),
                   jax.ShapeDtypeStruct((T, N_HEADS * HEAD_PAD), BF16),
                   jax.ShapeDtypeStruct((T, POOL_W), F32)],
        compiler_params=_cparams(1),
        name="proj",
    )(x, lw["ln1"], lw["w_in"], lw["q_norm"], lw["w_qa"], lw["w_qb"], lw["kv_norm"],
      lw["w_k"], lw["w_v"], tabs["cq"], tabs["sq"], tabs["ck"], tabs["sk"])


def _attn_kernel(q_ref, k_ref, v_ref, o_ref, m_scr, acc_scr, sa_scr, *, tk, n_chunks):
    m_scr[...] = jnp.full(m_scr.shape, -jnp.inf, F32)
    acc_scr[...] = jnp.zeros(acc_scr.shape, F32)

    def head_cols(hh):
        return slice(hh * HEAD_PAD, (hh + 1) * HEAD_PAD)

    def chunk(j):
        return pl.ds(j * tk if isinstance(j, int) else pl.multiple_of(j * tk, tk), tk)

    def scores(hh, j):
        kc = k_ref[chunk(j), head_cols(hh)]
        return lax.dot_general(q_ref[:, head_cols(hh)], kc, (((1,), (1,)), ((), ())),
                               preferred_element_type=F32)

    def update(hh, s, j):
        vc = v_ref[chunk(j), head_cols(hh)]
        m_old = m_scr[hh]
        m_new = jnp.maximum(m_old, jnp.max(s, axis=-1, keepdims=True))
        alpha = jnp.exp2(m_old - m_new)
        p = jnp.exp2(s - m_new).astype(BF16)
        acc_scr[hh] = alpha * acc_scr[hh] + jnp.dot(p, vc, preferred_element_type=F32)
        m_scr[hh] = m_new

    sa_scr[...] = scores(0, 0)

    def step(j, carry):
        sb = scores(1, j)
        update(0, sa_scr[...], j)
        sa_scr[...] = scores(0, j + 1)
        update(1, sb, j)
        return carry

    lax.fori_loop(0, n_chunks - 1, step, 0)
    sb = scores(1, n_chunks - 1)
    update(0, sa_scr[...], n_chunks - 1)
    update(1, sb, n_chunks - 1)
    outs = []
    for hh in range(2):
        acc = acc_scr[hh]
        outs.append(acc[:, :HEAD_V] / acc[:, HEAD_V:HEAD_V + 1])
    o_ref[...] = jnp.concatenate(outs, axis=-1).astype(o_ref.dtype)


def _attn_call(q, k, v, batch, seq_len, tq, tk):
    T = q.shape[0]
    n_q = seq_len // tq
    kern = functools.partial(_attn_kernel, tk=tk, n_chunks=seq_len // tk)
    pair = lambda rows: pl.BlockSpec((rows, 2 * HEAD_PAD), lambda b, hp, i: (b, hp))
    return pl.pallas_call(
        kern,
        grid=(batch, N_HEADS // 2, n_q),
        in_specs=[pl.BlockSpec((tq, 2 * HEAD_PAD), lambda b, hp, i: (b * n_q + i, hp)),
                  pair(seq_len), pair(seq_len)],
        out_specs=pl.BlockSpec((tq, 2 * HEAD_V), lambda b, hp, i: (b * n_q + i, hp)),
        out_shape=jax.ShapeDtypeStruct((T, ATT_W), BF16),
        scratch_shapes=[pltpu.VMEM((2, tq, 1), F32), pltpu.VMEM((2, tq, HEAD_PAD), F32),
                        pltpu.VMEM((tq, tk), F32)],
        compiler_params=_cparams(3),
        name="attn",
    )(q, k, v)


def _mix_kernel(x_ref, o_ref, u_ref, up_ref, un_ref, an_ref, pn_ref, wpool_ref, pscale_ref,
                wout_ref, ln2_ref, wrh_ref, wrl_ref, br_ref,
                xn_ref, h2_ref, idx_ref, gate_ref, ubuf, *, tm, seq_len):
    i = pl.program_id(0)
    tiles_per_seq = seq_len // tm
    ti = i % tiles_per_seq
    prev_ok = jnp.where(ti > 0, 1.0, 0.0).astype(F32)
    next_ok = jnp.where(ti < tiles_per_seq - 1, 1.0, 0.0).astype(F32)
    ubuf[0:POOL_HALO, :] = up_ref[...] * prev_ok
    ubuf[POOL_HALO:POOL_HALO + tm, :] = u_ref[...]
    ubuf[POOL_HALO + tm:2 * POOL_HALO + tm, :] = un_ref[...] * next_ok

    pos = ti * tm + lax.broadcasted_iota(I32, (tm, 1), 0)
    pooled = []
    for g, w in enumerate(POOL_WINDOWS):
        cols = slice(g * POOL_GW, (g + 1) * POOL_GW)
        tot = jnp.zeros((tm, POOL_GW), F32)
        for d in range(-(w // 2), w - w // 2):
            tot = tot + ubuf[POOL_HALO + d:POOL_HALO + d + tm, cols]
        lo = jnp.maximum(pos - w // 2, 0)
        hi = jnp.minimum(pos + (w - w // 2), seq_len)
        cnt = (hi - lo).astype(F32)
        pg = (tot / cnt - u_ref[:, cols]).astype(BF16)
        yg = jnp.dot(pg, wpool_ref[g], preferred_element_type=F32) * pscale_ref[:, cols]
        pooled.append(yg)
    pool = jnp.concatenate(pooled, axis=-1)
    attn = o_ref[...].astype(F32)
    mixed = jnp.concatenate([_rms(attn, an_ref[...]), _rms(pool, pn_ref[...])],
                            axis=-1).astype(BF16)
    xn = x_ref[...] + jnp.dot(mixed, wout_ref[...], preferred_element_type=F32)
    xn_ref[...] = xn
    h2 = _rms(xn, ln2_ref[...])
    h2_ref[...] = h2
    hi_ = h2.astype(BF16)
    lo_ = (h2 - hi_.astype(F32)).astype(BF16)
    logits = (jnp.dot(hi_, wrh_ref[...], preferred_element_type=F32)
              + jnp.dot(lo_, wrh_ref[...], preferred_element_type=F32)
              + jnp.dot(hi_, wrl_ref[...], preferred_element_type=F32)) + br_ref[...]
    lane = lax.broadcasted_iota(I32, logits.shape, 1)
    vals, idxs = [], []
    for _ in range(TOP_K):
        mx = jnp.max(logits, axis=-1, keepdims=True)
        ix = jnp.min(jnp.where(logits == mx, lane, ROUTER_PAD), axis=-1, keepdims=True)
        vals.append(mx)
        idxs.append(ix)
        logits = jnp.where(lane == ix, -jnp.inf, logits)
    es = [jnp.exp(vk - vals[0]) for vk in vals]
    den = es[0] + es[1] + es[2] + es[3]
    idx_out = jnp.zeros(lane.shape, I32)
    gate_out = jnp.zeros(lane.shape, F32)
    for kk in range(TOP_K):
        idx_out = jnp.where(lane == kk, idxs[kk], idx_out)
        gate_out = jnp.where(lane == kk, es[kk] / den, gate_out)
    idx_ref[...] = idx_out
    gate_ref[...] = gate_out


def _mix_call(x, o, u, lw, seq_len, tm):
    T = x.shape[0]
    hb = tm // POOL_HALO
    n_hblk = T // POOL_HALO
    full = lambda shape: pl.BlockSpec(shape, lambda i: (0,) * len(shape))
    row = lambda w: pl.BlockSpec((tm, w), lambda i: (i, 0))
    kern = functools.partial(_mix_kernel, tm=tm, seq_len=seq_len)
    return pl.pallas_call(
        kern,
        grid=(T // tm,),
        in_specs=[row(D_MODEL), row(ATT_W), row(POOL_W),
                  pl.BlockSpec((POOL_HALO, POOL_W), lambda i: (jnp.maximum(i * hb - 1, 0), 0)),
                  pl.BlockSpec((POOL_HALO, POOL_W),
                               lambda i: (jnp.minimum((i + 1) * hb, n_hblk - 1), 0)),
                  full((1, ATT_W)), full((1, POOL_W)), full((4, POOL_GW, POOL_GW)),
                  full((1, POOL_W)), full((ATT_W + POOL_W, D_MODEL)), full((1, D_MODEL)),
                  full((D_MODEL, ROUTER_PAD)), full((D_MODEL, ROUTER_PAD)),
                  full((1, ROUTER_PAD))],
        out_specs=[row(D_MODEL), row(D_MODEL), row(ROUTER_PAD), row(ROUTER_PAD)],
        out_shape=[jax.ShapeDtypeStruct((T, D_MODEL), F32),
                   jax.ShapeDtypeStruct((T, D_MODEL), F32),
                   jax.ShapeDtypeStruct((T, ROUTER_PAD), I32),
                   jax.ShapeDtypeStruct((T, ROUTER_PAD), F32)],
        scratch_shapes=[pltpu.VMEM((tm + 2 * POOL_HALO, POOL_W), F32)],
        compiler_params=_cparams(1),
        name="mix",
    )(x, o, u, u, u, lw["attn_out_norm"], lw["pool_out_norm"], lw["w_pool"], lw["pool_scale"],
      lw["w_out"], lw["ln2"], lw["w_r_hi"], lw["w_r_lo"], lw["b_r"])


def _dispatch_kernel(dest_ref, h_ref, xs_in_ref, xs_ref, sem, *, tm):
    del xs_in_ref

    def issue(r, c):
        for kk in range(TOP_K):
            d = dest_ref[r * TOP_K + kk]
            pltpu.make_async_copy(h_ref.at[pl.ds(r, 1)], xs_ref.at[pl.ds(d, 1)], sem).start()
        return c

    lax.fori_loop(0, tm, issue, 0)
    for _ in range(TOP_K):
        pltpu.make_async_copy(h_ref, xs_ref.at[pl.ds(0, tm)], sem).wait()


def _dispatch_call(h2, dest_flat, xs_init, tm):
    T = h2.shape[0]
    kern = functools.partial(_dispatch_kernel, tm=tm)
    return pl.pallas_call(
        kern,
        grid=(T // tm,),
        in_specs=[pl.BlockSpec((tm * TOP_K,), lambda i: (i,), memory_space=pltpu.SMEM),
                  pl.BlockSpec((tm, D_MODEL), lambda i: (i, 0)),
                  pl.BlockSpec(memory_space=pl.ANY)],
        out_specs=pl.BlockSpec(memory_space=pl.ANY),
        out_shape=jax.ShapeDtypeStruct(xs_init.shape, xs_init.dtype),
        scratch_shapes=[pltpu.SemaphoreType.DMA],
        input_output_aliases={2: 0},
        compiler_params=_cparams(1),
        name="dispatch",
    )(dest_flat, h2, xs_init)


def _ffn_kernel(be_ref, nu_ref, xs_ref, wg_ref, bg_ref, wu_ref, bu_ref, wd_ref, bd_ref, y_ref):
    del be_ref

    @pl.when(pl.program_id(0) < nu_ref[0])
    def _():
        x = xs_ref[...].astype(BF16)
        gt = jnp.minimum(jnp.dot(x, wg_ref[0], preferred_element_type=F32) + bg_ref[0],
                         SWIGLU_LIMIT)
        up = jnp.clip(jnp.dot(x, wu_ref[0], preferred_element_type=F32) + bu_ref[0],
                      -SWIGLU_LIMIT, SWIGLU_LIMIT)
        act = gt * jax.nn.sigmoid(SWIGLU_ALPHA * gt) * (up + 1.0)
        y_ref[...] = jnp.dot(act.astype(BF16), wd_ref[0], preferred_element_type=F32) + bd_ref[0]

    @pl.when(pl.program_id(0) >= nu_ref[0])
    def _():
        y_ref[...] = jnp.zeros(y_ref.shape, y_ref.dtype)


def _ffn_call(xs, block_expert, n_used, lw, bm):
    P = xs.shape[0]
    n_blocks = P // bm
    xmap = lambda i, be, nu: (jnp.minimum(i, nu[0] - 1), 0)
    ymap = lambda i, be, nu: (i, 0)
    wmap = lambda i, be, nu: (be[i], 0, 0)
    grid_spec = pltpu.PrefetchScalarGridSpec(
        num_scalar_prefetch=2,
        grid=(n_blocks,),
        in_specs=[pl.BlockSpec((bm, D_MODEL), xmap),
                  pl.BlockSpec((1, D_MODEL, D_MODEL), wmap), pl.BlockSpec((1, 1, D_MODEL), wmap),
                  pl.BlockSpec((1, D_MODEL, D_MODEL), wmap), pl.BlockSpec((1, 1, D_MODEL), wmap),
                  pl.BlockSpec((1, D_MODEL, D_MODEL), wmap), pl.BlockSpec((1, 1, D_MODEL), wmap)],
        out_specs=pl.BlockSpec((bm, D_MODEL), ymap),
    )
    return pl.pallas_call(
        _ffn_kernel,
        grid_spec=grid_spec,
        out_shape=jax.ShapeDtypeStruct((P, D_MODEL), F32),
        compiler_params=_cparams(1),
        name="ffn",
    )(block_expert, n_used, xs, lw["w_gate"], lw["b_gate"], lw["w_up"], lw["b_up"],
      lw["w_down"], lw["b_down"])


def _combine_kernel(dest_ref, x_ref, gate_ref, fn_ref, y_ref, out_ref, ybuf, sem, *, tm, final):
    def issue(r, c):
        for kk in range(TOP_K):
            d = dest_ref[r * TOP_K + kk]
            pltpu.make_async_copy(y_ref.at[pl.ds(d, 1)], ybuf.at[kk, pl.ds(r, 1)], sem).start()
        return c

    lax.fori_loop(0, tm, issue, 0)
    for kk in range(TOP_K):
        pltpu.make_async_copy(y_ref.at[pl.ds(0, tm)], ybuf.at[kk], sem).wait()
    g = gate_ref[...]
    acc = x_ref[...]
    for kk in range(TOP_K):
        acc = acc + ybuf[kk] * g[:, kk:kk + 1]
    if final:
        acc = _rms(acc, fn_ref[...])
    out_ref[...] = acc


def _combine_call(x, gates, dest_flat, y, final_norm, tm, final):
    T = x.shape[0]
    kern = functools.partial(_combine_kernel, tm=tm, final=final)
    return pl.pallas_call(
        kern,
        grid=(T // tm,),
        in_specs=[pl.BlockSpec((tm * TOP_K,), lambda i: (i,), memory_space=pltpu.SMEM),
                  pl.BlockSpec((tm, D_MODEL), lambda i: (i, 0)),
                  pl.BlockSpec((tm, ROUTER_PAD), lambda i: (i, 0)),
                  pl.BlockSpec((1, D_MODEL), lambda i: (0, 0)),
                  pl.BlockSpec(memory_space=pl.ANY)],
        out_specs=pl.BlockSpec((tm, D_MODEL), lambda i: (i, 0)),
        out_shape=jax.ShapeDtypeStruct((T, D_MODEL), F32),
        scratch_shapes=[pltpu.VMEM((TOP_K, tm, D_MODEL), F32), pltpu.SemaphoreType.DMA],
        compiler_params=_cparams(1),
        name="combine",
    )(dest_flat, x, gates, final_norm, y)


def _routing_tables(top_idx, bm):
    T = top_idx.shape[0]
    onehot = (top_idx[:, :, None] == jnp.arange(N_EXPERTS, dtype=I32)[None, None, :])
    per_tok = jnp.sum(onehot.astype(I32), axis=1)
    incl = jnp.cumsum(per_tok, axis=0)
    rank = jnp.take_along_axis(incl - per_tok, top_idx, axis=1)
    counts = incl[-1]
    padded = (counts + bm - 1) // bm * bm
    pend = jnp.cumsum(padded)
    pstart = pend - padded
    dest = pstart[top_idx] + rank
    n_blocks = (T * TOP_K + bm - 1) // bm + N_EXPERTS
    block_expert = jnp.minimum(
        jnp.searchsorted(pend, jnp.arange(n_blocks, dtype=I32) * bm, side="right"),
        N_EXPERTS - 1).astype(I32)
    n_used = (pend[-1] // bm).astype(I32).reshape(1)
    return dest.reshape(-1).astype(I32), block_expert, n_used, n_blocks


def _rope_tables(seq_len):
    inv = 1.0 / (ROPE_THETA ** (jnp.arange(0, HEAD_ROPE, 2, dtype=F32) / HEAD_ROPE))
    ang = jnp.arange(seq_len, dtype=F32)[:, None] * inv[None, :]
    cos, sin = jnp.cos(ang), jnp.sin(ang)
    scale = np.float32(np.log2(np.e) / np.sqrt(HEAD_NOPE + HEAD_ROPE))
    z = jnp.zeros((seq_len, HEAD_PAD - HEAD_NOPE - HEAD_ROPE), F32)
    ones = jnp.ones((seq_len, HEAD_NOPE), F32)
    zeros = jnp.zeros((seq_len, HEAD_NOPE), F32)
    ck = jnp.concatenate([zeros, cos, cos, z], axis=1)
    sk = jnp.concatenate([zeros, sin, sin, z], axis=1)
    cq = jnp.concatenate([ones, cos, cos, z], axis=1) * scale
    sq = sk * scale
    return {"cq": cq, "sq": sq, "ck": ck, "sk": sk}


def _pad_heads(w_nope, w_r1, w_r2):
    lead = w_nope.shape[:-2]
    z = jnp.zeros(lead + (N_HEADS, HEAD_PAD - HEAD_NOPE - HEAD_ROPE), w_nope.dtype)
    return jnp.concatenate([w_nope, w_r1, w_r2, z], axis=-1).reshape(lead + (N_HEADS * HEAD_PAD,))


def _prep_layer(l, ln1, w_in, q_norm, w_uq, kv_norm, w_ukv, w_pool, pool_scale, attn_out_norm,
                pool_out_norm, w_out, ln2, w_router, b_router, w_gate, b_gate, w_up, b_up,
                w_down, b_down):
    half = HEAD_ROPE // 2
    wi = w_in[l]
    kr = wi[:, Q_LORA + KV_LORA:Q_LORA + KV_LORA + HEAD_ROPE]
    kr1, kr2 = kr[:, :half], kr[:, half:]
    zn = jnp.zeros((D_MODEL, HEAD_NOPE), F32)
    zp = jnp.zeros((D_MODEL, HEAD_PAD - HEAD_NOPE - HEAD_ROPE), F32)
    kra = jnp.concatenate([zn, kr1, kr2, zp], axis=1)
    krb = jnp.concatenate([zn, -kr2, kr1, zp], axis=1)
    w_in_ext = jnp.concatenate(
        [wi[:, :Q_LORA + KV_LORA], kra, krb, wi[:, Q_LORA + KV_LORA + HEAD_ROPE:]], axis=1)
    wq = w_uq[l].reshape(Q_LORA, N_HEADS, HEAD_NOPE + HEAD_ROPE)
    qn_, q1, q2 = wq[..., :HEAD_NOPE], wq[..., HEAD_NOPE:HEAD_NOPE + half], wq[..., HEAD_NOPE + half:]
    w_qa = _pad_heads(qn_, q1, q2)
    w_qb = _pad_heads(jnp.zeros_like(qn_), -q2, q1)
    wkv = w_ukv[l].reshape(KV_LORA, N_HEADS, HEAD_NOPE + HEAD_V)
    kz = jnp.zeros((KV_LORA, N_HEADS, half), F32)
    w_k = _pad_heads(wkv[..., :HEAD_NOPE], kz, kz)
    w_v = jnp.pad(wkv[..., HEAD_NOPE:], ((0, 0), (0, 0), (0, HEAD_PAD - HEAD_V))).reshape(
        KV_LORA, N_HEADS * HEAD_PAD)
    wr = jnp.pad(w_router[l], ((0, 0), (0, ROUTER_PAD - N_EXPERTS)))
    wr_hi = wr.astype(BF16)
    wr_lo = (wr - wr_hi.astype(F32)).astype(BF16)
    br = jnp.concatenate([b_router[l].astype(F32),
                          jnp.full((ROUTER_PAD - N_EXPERTS,), -jnp.inf, F32)]).reshape(1, ROUTER_PAD)
    return {
        "ln1": ln1[l].reshape(1, -1), "w_in": w_in_ext.astype(BF16),
        "q_norm": q_norm[l].reshape(1, -1), "w_qa": w_qa.astype(BF16), "w_qb": w_qb.astype(BF16),
        "kv_norm": kv_norm[l].reshape(1, -1), "w_k": w_k.astype(BF16), "w_v": w_v.astype(BF16),
        "w_pool": w_pool[l].astype(BF16), "pool_scale": pool_scale[l].reshape(1, -1),
        "attn_out_norm": attn_out_norm[l].reshape(1, -1),
        "pool_out_norm": pool_out_norm[l].reshape(1, -1),
        "w_out": w_out[l].astype(BF16), "ln2": ln2[l].reshape(1, -1),
        "w_r_hi": wr_hi, "w_r_lo": wr_lo, "b_r": br,
        "w_gate": w_gate[l].astype(BF16), "b_gate": b_gate[l].reshape(N_EXPERTS, 1, -1),
        "w_up": w_up[l].astype(BF16), "b_up": b_up[l].reshape(N_EXPERTS, 1, -1),
        "w_down": w_down[l].astype(BF16), "b_down": b_down[l].reshape(N_EXPERTS, 1, -1),
    }


def _tiles(seq_len):
    tm = min(512, seq_len)
    tq = min(512, seq_len)
    tk = min(2048, seq_len)
    tc = min(256, seq_len)
    bm = 512
    return tm, tq, tk, tc, bm


def _trunk(x3, layers, final_norm):
    B, S, _ = x3.shape
    T = B * S
    tm, tq, tk, tc, bm = _tiles(S)
    tabs = _rope_tables(S)
    x = x3.reshape(T, D_MODEL)
    fn = final_norm.reshape(1, -1)
    for li, lw in enumerate(layers):
        q, k, v, u = _proj_call(x, lw, tabs, S, tm)
        o = _attn_call(q, k, v, B, S, tq, tk)
        x, h2, idx_pad, gate_pad = _mix_call(x, o, u, lw, S, tm)
        dest, block_expert, n_used, n_blocks = _routing_tables(idx_pad[:, :TOP_K], bm)
        xs = _dispatch_call(h2, dest, jnp.zeros((n_blocks * bm, D_MODEL), F32), tc)
        y = _ffn_call(xs, block_expert, n_used, lw, bm)
        x = _combine_call(x, gate_pad, dest, y, fn, tc, final=(li == len(layers) - 1))
    return x.reshape(B, S, D_MODEL)


def kernel(x_prompt, x_sample, ln1, w_in, q_norm, w_uq, kv_norm, w_ukv, w_pool, pool_scale,
           attn_out_norm, pool_out_norm, w_out, ln2, w_router, b_router, w_gate, b_gate, w_up,
           b_up, w_down, b_down, final_norm):
    params = (ln1, w_in, q_norm, w_uq, kv_norm, w_ukv, w_pool, pool_scale, attn_out_norm,
              pool_out_norm, w_out, ln2, w_router, b_router, w_gate, b_gate, w_up, b_up,
              w_down, b_down)
    layers = [_prep_layer(l, *params) for l in range(ln1.shape[0])]
    return (_trunk(x_prompt, layers, final_norm), _trunk(x_sample, layers, final_norm))
```

```python
import functools

import jax
import jax.numpy as jnp
import numpy as np
from jax import lax
from jax.experimental import pallas as pl
from jax.experimental.pallas import tpu as pltpu

F32 = jnp.float32
BF16 = jnp.bfloat16
I32 = jnp.int32

D_MODEL = 1024
N_HEADS = 8
HEAD_NOPE = 64
HEAD_ROPE = 32
HEAD_V = 64
HEAD_PAD = 128
Q_LORA = 256
KV_LORA = 128
ATT_W = 512
POOL_W = 512
POOL_WINDOWS = (2, 4, 8, 16)
POOL_GW = 128
POOL_HALO = 8
ROPE_THETA = 10000.0
N_EXPERTS = 32
TOP_K = 4
SWIGLU_LIMIT = 7.0
SWIGLU_ALPHA = 1.702
NORM_EPS = 1e-5
ROUTER_PAD = 128
SUBLANES = 8

_C_Q = 0
_C_KV = _C_Q + Q_LORA
_C_KRA = _C_KV + KV_LORA
_C_KRB = _C_KRA + HEAD_PAD
_C_U = _C_KRB + HEAD_PAD
IN_W_EXT = _C_U + POOL_W

VMEM_LIMIT = 56 * 1024 * 1024


def _cparams(n_axes):
    return pltpu.CompilerParams(dimension_semantics=("arbitrary",) * n_axes,
                                vmem_limit_bytes=VMEM_LIMIT)


def _rms(x, g):
    return x * lax.rsqrt(jnp.mean(x * x, axis=-1, keepdims=True) + NORM_EPS) * g


def _proj_kernel(x_ref, ln1_ref, win_ref, qn_ref, wqa_ref, wqb_ref, kvn_ref, wk_ref, wv_ref,
                 cq_ref, sq_ref, ck_ref, sk_ref, q_ref, k_ref, v_ref, u_ref):
    h = _rms(x_ref[...], ln1_ref[...]).astype(BF16)
    proj = jnp.dot(h, win_ref[...], preferred_element_type=F32)
    u_ref[...] = proj[:, _C_U:_C_U + POOL_W]
    hq = _rms(proj[:, _C_Q:_C_Q + Q_LORA], qn_ref[...]).astype(BF16)
    qa = jnp.dot(hq, wqa_ref[...], preferred_element_type=F32)
    qb = jnp.dot(hq, wqb_ref[...], preferred_element_type=F32)
    hkv = _rms(proj[:, _C_KV:_C_KV + KV_LORA], kvn_ref[...]).astype(BF16)
    kn = jnp.dot(hkv, wk_ref[...], preferred_element_type=F32)
    vv = jnp.dot(hkv, wv_ref[...], preferred_element_type=F32)
    lane = lax.broadcasted_iota(I32, vv.shape, 1)
    v_ref[...] = jnp.where((lane & (HEAD_PAD - 1)) == HEAD_V, 1.0, vv).astype(BF16)
    k_rope = (proj[:, _C_KRA:_C_KRA + HEAD_PAD] * ck_ref[...]
              + proj[:, _C_KRB:_C_KRB + HEAD_PAD] * sk_ref[...])
    cq = cq_ref[...]
    sq = sq_ref[...]
    for hd in range(N_HEADS):
        sl = slice(hd * HEAD_PAD, (hd + 1) * HEAD_PAD)
        q_ref[:, sl] = (qa[:, sl] * cq + qb[:, sl] * sq).astype(BF16)
        k_ref[:, sl] = (kn[:, sl] + k_rope).astype(BF16)


def _proj_call(x, lw, tabs, seq_len, tm):
    T = x.shape[0]
    tiles_per_seq = seq_len // tm
    full = lambda shape: pl.BlockSpec(shape, lambda i: (0,) * len(shape))
    tab = pl.BlockSpec((tm, HEAD_PAD), lambda i: (i % tiles_per_seq, 0))
    row = lambda w: pl.BlockSpec((tm, w), lambda i: (i, 0))
    return pl.pallas_call(
        _proj_kernel,
        grid=(T // tm,),
        in_specs=[row(D_MODEL), full((1, D_MODEL)), full((D_MODEL, IN_W_EXT)),
                  full((1, Q_LORA)), full((Q_LORA, N_HEADS * HEAD_PAD)),
                  full((Q_LORA, N_HEADS * HEAD_PAD)),
                  full((1, KV_LORA)), full((KV_LORA, N_HEADS * HEAD_PAD)),
                  full((KV_LORA, N_HEADS * HEAD_PAD)), tab, tab, tab, tab],
        out_specs=[row(N_HEADS * HEAD_PAD), row(N_HEADS * HEAD_PAD), row(N_HEADS * HEAD_PAD),
                   row(POOL_W)],
        out_shape=[jax.ShapeDtypeStruct((T, N_HEADS * HEAD_PAD), BF16),
                   jax.ShapeDtypeStruct((T, N_HEADS * HEAD_PAD), BF16),
                   jax.ShapeDtypeStruct((T, N_HEADS * HEAD_PAD), BF16),
                   jax.ShapeDtypeStruct((T, POOL_W), F32)],
        compiler_params=_cparams(1),
        name="proj",
    )(x, lw["ln1"], lw["w_in"], lw["q_norm"], lw["w_qa"], lw["w_qb"], lw["kv_norm"],
      lw["w_k"], lw["w_v"], tabs["cq"], tabs["sq"], tabs["ck"], tabs["sk"])


def _attn_kernel(q_ref, k_ref, v_ref, o_ref, m_scr, acc_scr, sa_scr, *, tk, n_chunks):
    m_scr[...] = jnp.full(m_scr.shape, -jnp.inf, F32)
    acc_scr[...] = jnp.zeros(acc_scr.shape, F32)

    def head_cols(hh):
        return slice(hh * HEAD_PAD, (hh + 1) * HEAD_PAD)

    def chunk(j):
        return pl.ds(j * tk if isinstance(j, int) else pl.multiple_of(j * tk, tk), tk)

    def scores(hh, j):
        kc = k_ref[chunk(j), head_cols(hh)]
        return lax.dot_general(q_ref[:, head_cols(hh)], kc, (((1,), (1,)), ((), ())),
                               preferred_element_type=F32)

    def update(hh, s, j):
        vc = v_ref[chunk(j), head_cols(hh)]
        m_old = m_scr[hh]
        m_new = jnp.maximum(m_old, jnp.max(s, axis=-1, keepdims=True))
        alpha = jnp.exp2(m_old - m_new)
        p = jnp.exp2(s - m_new).astype(BF16)
        acc_scr[hh] = alpha * acc_scr[hh] + jnp.dot(p, vc, preferred_element_type=F32)
        m_scr[hh] = m_new

    sa_scr[...] = scores(0, 0)

    def step(j, carry):
        sb = scores(1, j)
        update(0, sa_scr[...], j)
        sa_scr[...] = scores(0, j + 1)
        update(1, sb, j)
        return carry

    lax.fori_loop(0, n_chunks - 1, step, 0)
    sb = scores(1, n_chunks - 1)
    update(0, sa_scr[...], n_chunks - 1)
    update(1, sb, n_chunks - 1)
    outs = []
    for hh in range(2):
        acc = acc_scr[hh]
        outs.append(acc[:, :HEAD_V] / acc[:, HEAD_V:HEAD_V + 1])
    o_ref[...] = jnp.concatenate(outs, axis=-1).astype(o_ref.dtype)


def _attn_call(q, k, v, batch, seq_len, tq, tk):
    T = q.shape[0]
    n_q = seq_len // tq
    kern = functools.partial(_attn_kernel, tk=tk, n_chunks=seq_len // tk)
    pair = lambda rows: pl.BlockSpec((rows, 2 * HEAD_PAD), lambda b, hp, i: (b, hp))
    return pl.pallas_call(
        kern,
        grid=(batch, N_HEADS // 2, n_q),
        in_specs=[pl.BlockSpec((tq, 2 * HEAD_PAD), lambda b, hp, i: (b * n_q + i, hp)),
                  pair(seq_len), pair(seq_len)],
        out_specs=pl.BlockSpec((tq, 2 * HEAD_V), lambda b, hp, i: (b * n_q + i, hp)),
        out_shape=jax.ShapeDtypeStruct((T, ATT_W), BF16),
        scratch_shapes=[pltpu.VMEM((2, tq, 1), F32), pltpu.VMEM((2, tq, HEAD_PAD), F32),
                        pltpu.VMEM((tq, tk), F32)],
        compiler_params=_cparams(3),
        name="attn",
    )(q, k, v)


def _mix_kernel(x_ref, o_ref, u_ref, up_ref, un_ref, an_ref, pn_ref, wpool_ref, pscale_ref,
                wout_ref, ln2_ref, wrh_ref, wrl_ref, br_ref,
                xn_ref, h2_ref, idx_ref, gate_ref, rank_ref, cnt_ref, ubuf, *, tm, seq_len):
    i = pl.program_id(0)
    tiles_per_seq = seq_len // tm
    ti = i % tiles_per_seq
    prev_ok = jnp.where(ti > 0, 1.0, 0.0).astype(F32)
    next_ok = jnp.where(ti < tiles_per_seq - 1, 1.0, 0.0).astype(F32)
    ubuf[0:POOL_HALO, :] = up_ref[...] * prev_ok
    ubuf[POOL_HALO:POOL_HALO + tm, :] = u_ref[...]
    ubuf[POOL_HALO + tm:2 * POOL_HALO + tm, :] = un_ref[...] * next_ok

    pos = ti * tm + lax.broadcasted_iota(I32, (tm, 1), 0)
    pooled = []
    for g, w in enumerate(POOL_WINDOWS):
        cols = slice(g * POOL_GW, (g + 1) * POOL_GW)
        tot = jnp.zeros((tm, POOL_GW), F32)
        for d in range(-(w // 2), w - w // 2):
            tot = tot + ubuf[POOL_HALO + d:POOL_HALO + d + tm, cols]
        lo = jnp.maximum(pos - w // 2, 0)
        hi = jnp.minimum(pos + (w - w // 2), seq_len)
        cnt = (hi - lo).astype(F32)
        pg = (tot / cnt - u_ref[:, cols]).astype(BF16)
        yg = jnp.dot(pg, wpool_ref[g], preferred_element_type=F32) * pscale_ref[:, cols]
        pooled.append(yg)
    pool = jnp.concatenate(pooled, axis=-1)
    attn = o_ref[...].astype(F32)
    mixed = jnp.concatenate([_rms(attn, an_ref[...]), _rms(pool, pn_ref[...])],
                            axis=-1).astype(BF16)
    xn = x_ref[...] + jnp.dot(mixed, wout_ref[...], preferred_element_type=F32)
    xn_ref[...] = xn
    h2 = _rms(xn, ln2_ref[...])
    h2_ref[...] = h2
    hi_ = h2.astype(BF16)
    lo_ = (h2 - hi_.astype(F32)).astype(BF16)
    logits = (jnp.dot(hi_, wrh_ref[...], preferred_element_type=F32)
              + jnp.dot(lo_, wrh_ref[...], preferred_element_type=F32)
              + jnp.dot(hi_, wrl_ref[...], preferred_element_type=F32)) + br_ref[...]
    lane = lax.broadcasted_iota(I32, logits.shape, 1)
    vals, idxs = [], []
    for _ in range(TOP_K):
        mx = jnp.max(logits, axis=-1, keepdims=True)
        ix = jnp.min(jnp.where(logits == mx, lane, ROUTER_PAD), axis=-1, keepdims=True)
        vals.append(mx)
        idxs.append(ix)
        logits = jnp.where(lane == ix, -jnp.inf, logits)
    es = [jnp.exp(vk - vals[0]) for vk in vals]
    den = es[0] + es[1] + es[2] + es[3]
    hits = [lane == idxs[kk] for kk in range(TOP_K)]
    routed = jnp.where(hits[0] | hits[1] | hits[2] | hits[3], 1.0, 0.0)
    r_i = lax.broadcasted_iota(I32, (tm, tm), 0)
    c_i = lax.broadcasted_iota(I32, (tm, tm), 1)
    ltri = jnp.where(c_i < r_i, 1.0, 0.0).astype(BF16)
    before = jnp.dot(ltri, routed.astype(BF16), preferred_element_type=F32)
    idx_out = jnp.zeros(lane.shape, I32)
    gate_out = jnp.zeros(lane.shape, F32)
    rank_out = jnp.zeros(lane.shape, I32)
    for kk in range(TOP_K):
        rk = jnp.sum(jnp.where(hits[kk], before, 0.0), axis=-1, keepdims=True).astype(I32)
        idx_out = jnp.where(lane == kk, idxs[kk], idx_out)
        gate_out = jnp.where(lane == kk, es[kk] / den, gate_out)
        rank_out = jnp.where(lane == kk, rk, rank_out)
    idx_ref[...] = idx_out
    gate_ref[...] = gate_out
    rank_ref[...] = rank_out
    counts = jnp.sum(routed, axis=0, keepdims=True).astype(I32)
    cnt_ref[0] = jnp.broadcast_to(counts, (SUBLANES, ROUTER_PAD))


def _mix_call(x, o, u, lw, seq_len, tm):
    T = x.shape[0]
    hb = tm // POOL_HALO
    n_hblk = T // POOL_HALO
    full = lambda shape: pl.BlockSpec(shape, lambda i: (0,) * len(shape))
    row = lambda w: pl.BlockSpec((tm, w), lambda i: (i, 0))
    kern = functools.partial(_mix_kernel, tm=tm, seq_len=seq_len)
    return pl.pallas_call(
        kern,
        grid=(T // tm,),
        in_specs=[row(D_MODEL), row(ATT_W), row(POOL_W),
                  pl.BlockSpec((POOL_HALO, POOL_W), lambda i: (jnp.maximum(i * hb - 1, 0), 0)),
                  pl.BlockSpec((POOL_HALO, POOL_W),
                               lambda i: (jnp.minimum((i + 1) * hb, n_hblk - 1), 0)),
                  full((1, ATT_W)), full((1, POOL_W)), full((4, POOL_GW, POOL_GW)),
                  full((1, POOL_W)), full((ATT_W + POOL_W, D_MODEL)), full((1, D_MODEL)),
                  full((D_MODEL, ROUTER_PAD)), full((D_MODEL, ROUTER_PAD)),
                  full((1, ROUTER_PAD))],
        out_specs=[row(D_MODEL), row(D_MODEL), row(ROUTER_PAD), row(ROUTER_PAD), row(ROUTER_PAD),
                   pl.BlockSpec((1, SUBLANES, ROUTER_PAD), lambda i: (i, 0, 0))],
        out_shape=[jax.ShapeDtypeStruct((T, D_MODEL), F32),
                   jax.ShapeDtypeStruct((T, D_MODEL), F32),
                   jax.ShapeDtypeStruct((T, ROUTER_PAD), I32),
                   jax.ShapeDtypeStruct((T, ROUTER_PAD), F32),
                   jax.ShapeDtypeStruct((T, ROUTER_PAD), I32),
                   jax.ShapeDtypeStruct((T // tm, SUBLANES, ROUTER_PAD), I32)],
        scratch_shapes=[pltpu.VMEM((tm + 2 * POOL_HALO, POOL_W), F32)],
        compiler_params=_cparams(1),
        name="mix",
    )(x, o, u, u, u, lw["attn_out_norm"], lw["pool_out_norm"], lw["w_pool"], lw["pool_scale"],
      lw["w_out"], lw["ln2"], lw["w_r_hi"], lw["w_r_lo"], lw["b_r"])


def _dispatch_kernel(pend_ref, nu_ref, gs_ref, cn_ref, dest_ref, h_ref, xs_ref, zbuf, sem, zsem,
                     *, tm, bm, n_blocks, n_runs):
    @pl.when(pl.program_id(0) == 0)
    def _():
        zbuf[...] = jnp.zeros(zbuf.shape, zbuf.dtype)

        def zero_chunk(row):
            return pltpu.make_async_copy(zbuf.at[pl.ds(0, SUBLANES)],
                                         xs_ref.at[pl.ds(row, SUBLANES)], zsem)

        def run_tail(t, n_started):
            def per_expert(e, n):
                cnt = cn_ref[t, e]
                last = gs_ref[t, e] + (cnt + SUBLANES - 1) // SUBLANES * SUBLANES - SUBLANES

                @pl.when(cnt > 0)
                def _():
                    zero_chunk(pl.multiple_of(last, SUBLANES)).start()
                return n + jnp.where(cnt > 0, 1, 0)
            return lax.fori_loop(0, N_EXPERTS, per_expert, n_started)

        n_chunks = lax.fori_loop(0, n_runs, run_tail, 0)

        def chunk_done(c, carry):
            zero_chunk(0).wait()
            return carry

        lax.fori_loop(0, n_chunks, chunk_done, 0)

        def zero_block(b):
            return pltpu.make_async_copy(zbuf, xs_ref.at[pl.ds(b * bm, bm)], zsem)

        def last_of_expert(e, c):
            prev_end = jnp.where(e > 0, pend_ref[jnp.maximum(e - 1, 0)], 0)

            @pl.when(pend_ref[e] > prev_end)
            def _():
                zero_block(pend_ref[e] // bm - 1).start()
                zero_block(0).wait()
            return c

        lax.fori_loop(0, N_EXPERTS, last_of_expert, 0)

        def tail_block(b, c):
            zero_block(b).start()
            zero_block(0).wait()
            return c

        lax.fori_loop(nu_ref[0], n_blocks, tail_block, 0)

    def issue(r, c):
        for kk in range(TOP_K):
            d = dest_ref[r * TOP_K + kk]
            pltpu.make_async_copy(h_ref.at[pl.ds(r, 1)], xs_ref.at[pl.ds(d, 1)], sem).start()
        return c

    lax.fori_loop(0, tm, issue, 0)
    for _ in range(TOP_K):
        pltpu.make_async_copy(h_ref, xs_ref.at[pl.ds(0, tm)], sem).wait()


def _dispatch_call(h2, dest_flat, pend, n_used, gstart, cnt, n_blocks, tm, bm):
    T = h2.shape[0]
    kern = functools.partial(_dispatch_kernel, tm=tm, bm=bm, n_blocks=n_blocks,
                             n_runs=gstart.shape[0])
    grid_spec = pltpu.PrefetchScalarGridSpec(
        num_scalar_prefetch=4,
        grid=(T // tm,),
        in_specs=[pl.BlockSpec((tm * TOP_K,), lambda i, *_: (i,), memory_space=pltpu.SMEM),
                  pl.BlockSpec((tm, D_MODEL), lambda i, *_: (i, 0))],
        out_specs=pl.BlockSpec(memory_space=pl.ANY),
        scratch_shapes=[pltpu.VMEM((bm, D_MODEL), F32), pltpu.SemaphoreType.DMA,
                        pltpu.SemaphoreType.DMA],
    )
    return pl.pallas_call(
        kern,
        grid_spec=grid_spec,
        out_shape=jax.ShapeDtypeStruct((n_blocks * bm, D_MODEL), F32),
        compiler_params=_cparams(1),
        name="dispatch",
    )(pend, n_used, gstart, cnt, dest_flat, h2)


def _ffn_kernel(be_ref, nu_ref, xs_ref, wg_ref, bg_ref, wu_ref, bu_ref, wd_ref, bd_ref, y_ref):
    del be_ref

    @pl.when(pl.program_id(0) < nu_ref[0])
    def _():
        x = xs_ref[...].astype(BF16)
        gt = jnp.minimum(jnp.dot(x, wg_ref[0], preferred_element_type=F32) + bg_ref[0],
                         SWIGLU_LIMIT)
        up = jnp.clip(jnp.dot(x, wu_ref[0], preferred_element_type=F32) + bu_ref[0],
                      -SWIGLU_LIMIT, SWIGLU_LIMIT)
        act = gt * jax.nn.sigmoid(SWIGLU_ALPHA * gt) * (up + 1.0)
        y_ref[...] = jnp.dot(act.astype(BF16), wd_ref[0], preferred_element_type=F32) + bd_ref[0]

    @pl.when(pl.program_id(0) >= nu_ref[0])
    def _():
        y_ref[...] = jnp.zeros(y_ref.shape, y_ref.dtype)


def _ffn_call(xs, block_expert, n_used, lw, bm):
    P = xs.shape[0]
    n_blocks = P // bm
    xmap = lambda i, be, nu: (jnp.minimum(i, nu[0] - 1), 0)
    ymap = lambda i, be, nu: (i, 0)
    wmap = lambda i, be, nu: (be[i], 0, 0)
    grid_spec = pltpu.PrefetchScalarGridSpec(
        num_scalar_prefetch=2,
        grid=(n_blocks,),
        in_specs=[pl.BlockSpec((bm, D_MODEL), xmap),
                  pl.BlockSpec((1, D_MODEL, D_MODEL), wmap), pl.BlockSpec((1, 1, D_MODEL), wmap),
                  pl.BlockSpec((1, D_MODEL, D_MODEL), wmap), pl.BlockSpec((1, 1, D_MODEL), wmap),
                  pl.BlockSpec((1, D_MODEL, D_MODEL), wmap), pl.BlockSpec((1, 1, D_MODEL), wmap)],
        out_specs=pl.BlockSpec((bm, D_MODEL), ymap),
    )
    return pl.pallas_call(
        _ffn_kernel,
        grid_spec=grid_spec,
        out_shape=jax.ShapeDtypeStruct((P, D_MODEL), F32),
        compiler_params=_cparams(1),
        name="ffn",
    )(block_expert, n_used, xs, lw["w_gate"], lw["b_gate"], lw["w_up"], lw["b_up"],
      lw["w_down"], lw["b_down"])


def _combine_kernel(gs_ref, cn_ref, x_ref, idx_ref, rank_ref, gate_ref, cnt_ref, fn_ref, y_ref,
                    out_ref, ylocal, sem, *, tm, lrows, final):
    @pl.when(pl.program_id(0) == 0)
    def _():
        ylocal[...] = jnp.zeros(ylocal.shape, ylocal.dtype)

    def chunk_copy(src_row, dst_row):
        return pltpu.make_async_copy(y_ref.at[pl.ds(src_row, SUBLANES)],
                                     ylocal.at[pl.ds(dst_row, SUBLANES)], sem)

    def per_expert(e, off):
        n_ch = (cn_ref[0, 0, e] + SUBLANES - 1) // SUBLANES
        src = gs_ref[0, 0, e]

        def per_chunk(c, carry):
            chunk_copy(pl.multiple_of(src + c * SUBLANES, SUBLANES),
                       pl.multiple_of(off + c * SUBLANES, SUBLANES)).start()
            return carry

        lax.fori_loop(0, n_ch, per_chunk, 0)
        return off + n_ch * SUBLANES

    used = lax.fori_loop(0, N_EXPERTS, per_expert, 0)

    def wait_chunk(c, carry):
        chunk_copy(0, 0).wait()
        return carry

    lax.fori_loop(0, used // SUBLANES, wait_chunk, 0)

    n_ch_e = ((cnt_ref[0] + SUBLANES - 1) // SUBLANES).astype(F32)
    e_r = lax.broadcasted_iota(I32, (ROUTER_PAD, ROUTER_PAD), 0)
    e_c = lax.broadcasted_iota(I32, (ROUTER_PAD, ROUTER_PAD), 1)
    before = jnp.where(e_r < e_c, 1.0, 0.0).astype(BF16)
    run_start = jnp.dot(n_ch_e.astype(BF16), before, preferred_element_type=F32)[0:1, :]
    run_start = run_start * float(SUBLANES)
    lane = lax.broadcasted_iota(I32, (tm, ROUTER_PAD), 1)
    idx = idx_ref[...]
    rank = rank_ref[...]
    gates = gate_ref[...]
    r_iota = lax.broadcasted_iota(I32, (tm, lrows), 1)
    gmat = jnp.zeros((tm, lrows), F32)
    for kk in range(TOP_K):
        e_k = jnp.sum(jnp.where(lane == kk, idx, 0), axis=-1, keepdims=True)
        start_k = jnp.sum(jnp.where(lane == e_k, run_start, 0.0), axis=-1, keepdims=True)
        rank_k = jnp.sum(jnp.where(lane == kk, rank, 0), axis=-1, keepdims=True)
        row_k = start_k.astype(I32) + rank_k
        g_k = jnp.sum(jnp.where(lane == kk, gates, 0.0), axis=-1, keepdims=True)
        gmat = jnp.where(r_iota == row_k, g_k, gmat)
    moe = jnp.dot(gmat.astype(BF16), ylocal[...].astype(BF16), preferred_element_type=F32)
    acc = x_ref[...] + moe
    if final:
        acc = _rms(acc, fn_ref[...])
    out_ref[...] = acc


def _combine_call(x, idx_pad, rank_pad, gate_pad, tile_cnt, gstart, y, final_norm, tm, final):
    T = x.shape[0]
    n_tiles = T // tm
    lrows = tm * TOP_K + N_EXPERTS * SUBLANES
    tab = lambda a: jnp.pad(a, ((0, 0), (0, ROUTER_PAD - N_EXPERTS))).reshape(
        n_tiles, 1, ROUTER_PAD)
    kern = functools.partial(_combine_kernel, tm=tm, lrows=lrows, final=final)
    stab = pl.BlockSpec((1, 1, ROUTER_PAD), lambda i: (i, 0, 0), memory_space=pltpu.SMEM)
    row = lambda w: pl.BlockSpec((tm, w), lambda i: (i, 0))
    return pl.pallas_call(
        kern,
        grid=(n_tiles,),
        in_specs=[stab, stab, row(D_MODEL), row(ROUTER_PAD), row(ROUTER_PAD), row(ROUTER_PAD),
                  pl.BlockSpec((1, SUBLANES, ROUTER_PAD), lambda i: (i, 0, 0)),
                  pl.BlockSpec((1, D_MODEL), lambda i: (0, 0)),
                  pl.BlockSpec(memory_space=pl.ANY)],
        out_specs=row(D_MODEL),
        out_shape=jax.ShapeDtypeStruct((T, D_MODEL), F32),
        scratch_shapes=[pltpu.VMEM((lrows, D_MODEL), F32), pltpu.SemaphoreType.DMA],
        compiler_params=_cparams(1),
        name="combine",
    )(tab(gstart), tab(tile_cnt[:, 0, :N_EXPERTS]), x, idx_pad, rank_pad, gate_pad, tile_cnt,
      final_norm, y)


def _routing_tables(top_idx, rank, tile_cnt, tm, bm):
    T = top_idx.shape[0]
    n_tiles = T // tm
    cnt = tile_cnt[:, 0, :N_EXPERTS]
    run = (cnt + SUBLANES - 1) // SUBLANES * SUBLANES
    padded = (jnp.sum(run, axis=0) + bm - 1) // bm * bm
    pend = jnp.cumsum(padded).astype(I32)
    pstart = pend - padded
    gstart = (pstart[None, :] + jnp.cumsum(run, axis=0) - run).astype(I32)
    experts = jnp.arange(N_EXPERTS, dtype=I32)
    hit = top_idx.reshape(n_tiles, tm, TOP_K, 1) == experts
    base = jnp.sum(jnp.where(hit, gstart[:, None, None, :], 0), axis=-1)
    dest = base.reshape(T, TOP_K) + rank
    max_rows = T * TOP_K + n_tiles * N_EXPERTS * (SUBLANES - 1)
    n_blocks = (max_rows + bm - 1) // bm + N_EXPERTS
    block_expert = jnp.minimum(
        jnp.searchsorted(pend, jnp.arange(n_blocks, dtype=I32) * bm, side="right"),
        N_EXPERTS - 1).astype(I32)
    n_used = (pend[-1] // bm).astype(I32).reshape(1)
    return dest.reshape(-1).astype(I32), gstart, cnt, pend, block_expert, n_used, n_blocks


def _rope_tables(seq_len):
    inv = 1.0 / (ROPE_THETA ** (jnp.arange(0, HEAD_ROPE, 2, dtype=F32) / HEAD_ROPE))
    ang = jnp.arange(seq_len, dtype=F32)[:, None] * inv[None, :]
    cos, sin = jnp.cos(ang), jnp.sin(ang)
    scale = np.float32(np.log2(np.e) / np.sqrt(HEAD_NOPE + HEAD_ROPE))
    z = jnp.zeros((seq_len, HEAD_PAD - HEAD_NOPE - HEAD_ROPE), F32)
    ones = jnp.ones((seq_len, HEAD_NOPE), F32)
    zeros = jnp.zeros((seq_len, HEAD_NOPE), F32)
    ck = jnp.concatenate([zeros, cos, cos, z], axis=1)
    sk = jnp.concatenate([zeros, sin, sin, z], axis=1)
    cq = jnp.concatenate([ones, cos, cos, z], axis=1) * scale
    sq = sk * scale
    return {"cq": cq, "sq": sq, "ck": ck, "sk": sk}


def _pad_heads(w_nope, w_r1, w_r2):
    lead = w_nope.shape[:-2]
    z = jnp.zeros(lead + (N_HEADS, HEAD_PAD - HEAD_NOPE - HEAD_ROPE), w_nope.dtype)
    return jnp.concatenate([w_nope, w_r1, w_r2, z], axis=-1).reshape(lead + (N_HEADS * HEAD_PAD,))


def _prep_layer(l, ln1, w_in, q_norm, w_uq, kv_norm, w_ukv, w_pool, pool_scale, attn_out_norm,
                pool_out_norm, w_out, ln2, w_router, b_router, w_gate, b_gate, w_up, b_up,
                w_down, b_down):
    half = HEAD_ROPE // 2
    wi = w_in[l]
    kr = wi[:, Q_LORA + KV_LORA:Q_LORA + KV_LORA + HEAD_ROPE]
    kr1, kr2 = kr[:, :half], kr[:, half:]
    zn = jnp.zeros((D_MODEL, HEAD_NOPE), F32)
    zp = jnp.zeros((D_MODEL, HEAD_PAD - HEAD_NOPE - HEAD_ROPE), F32)
    kra = jnp.concatenate([zn, kr1, kr2, zp], axis=1)
    krb = jnp.concatenate([zn, -kr2, kr1, zp], axis=1)
    w_in_ext = jnp.concatenate(
        [wi[:, :Q_LORA + KV_LORA], kra, krb, wi[:, Q_LORA + KV_LORA + HEAD_ROPE:]], axis=1)
    wq = w_uq[l].reshape(Q_LORA, N_HEADS, HEAD_NOPE + HEAD_ROPE)
    qn_, q1, q2 = wq[..., :HEAD_NOPE], wq[..., HEAD_NOPE:HEAD_NOPE + half], wq[..., HEAD_NOPE + half:]
    w_qa = _pad_heads(qn_, q1, q2)
    w_qb = _pad_heads(jnp.zeros_like(qn_), -q2, q1)
    wkv = w_ukv[l].reshape(KV_LORA, N_HEADS, HEAD_NOPE + HEAD_V)
    kz = jnp.zeros((KV_LORA, N_HEADS, half), F32)
    w_k = _pad_heads(wkv[..., :HEAD_NOPE], kz, kz)
    w_v = jnp.pad(wkv[..., HEAD_NOPE:], ((0, 0), (0, 0), (0, HEAD_PAD - HEAD_V))).reshape(
        KV_LORA, N_HEADS * HEAD_PAD)
    wr = jnp.pad(w_router[l], ((0, 0), (0, ROUTER_PAD - N_EXPERTS)))
    wr_hi = wr.astype(BF16)
    wr_lo = (wr - wr_hi.astype(F32)).astype(BF16)
    br = jnp.concatenate([b_router[l].astype(F32),
                          jnp.full((ROUTER_PAD - N_EXPERTS,), -jnp.inf, F32)]).reshape(1, ROUTER_PAD)
    return {
        "ln1": ln1[l].reshape(1, -1), "w_in": w_in_ext.astype(BF16),
        "q_norm": q_norm[l].reshape(1, -1), "w_qa": w_qa.astype(BF16), "w_qb": w_qb.astype(BF16),
        "kv_norm": kv_norm[l].reshape(1, -1), "w_k": w_k.astype(BF16), "w_v": w_v.astype(BF16),
        "w_pool": w_pool[l].astype(BF16), "pool_scale": pool_scale[l].reshape(1, -1),
        "attn_out_norm": attn_out_norm[l].reshape(1, -1),
        "pool_out_norm": pool_out_norm[l].reshape(1, -1),
        "w_out": w_out[l].astype(BF16), "ln2": ln2[l].reshape(1, -1),
        "w_r_hi": wr_hi, "w_r_lo": wr_lo, "b_r": br,
        "w_gate": w_gate[l].astype(BF16), "b_gate": b_gate[l].reshape(N_EXPERTS, 1, -1),
        "w_up": w_up[l].astype(BF16), "b_up": b_up[l].reshape(N_EXPERTS, 1, -1),
        "w_down": w_down[l].astype(BF16), "b_down": b_down[l].reshape(N_EXPERTS, 1, -1),
    }


def _tiles(seq_len):
    tm = min(512, seq_len)
    tq = min(512, seq_len)
    tk = min(2048, seq_len)
    tc = min(256, seq_len)
    bm = 512
    return tm, tq, tk, tc, bm


def _trunk(x3, layers, final_norm):
    B, S, _ = x3.shape
    T = B * S
    tm, tq, tk, tc, bm = _tiles(S)
    tabs = _rope_tables(S)
    x = x3.reshape(T, D_MODEL)
    fn = final_norm.reshape(1, -1)
    for li, lw in enumerate(layers):
        q, k, v, u = _proj_call(x, lw, tabs, S, tm)
        o = _attn_call(q, k, v, B, S, tq, tk)
        x, h2, idx_pad, gate_pad, rank_pad, tile_cnt = _mix_call(x, o, u, lw, S, tm)
        dest, gstart, cnt, pend, block_expert, n_used, n_blocks = _routing_tables(
            idx_pad[:, :TOP_K], rank_pad[:, :TOP_K], tile_cnt, tm, bm)
        xs = _dispatch_call(h2, dest, pend, n_used, gstart, cnt, n_blocks, tc, bm)
        y = _ffn_call(xs, block_expert, n_used, lw, bm)
        x = _combine_call(x, idx_pad, rank_pad, gate_pad, tile_cnt, gstart, y, fn, tm,
                          final=(li == len(layers) - 1))
    return x.reshape(B, S, D_MODEL)


def kernel(x_prompt, x_sample, ln1, w_in, q_norm, w_uq, kv_norm, w_ukv, w_pool, pool_scale,
           attn_out_norm, pool_out_norm, w_out, ln2, w_router, b_router, w_gate, b_gate, w_up,
           b_up, w_down, b_down, final_norm):
    params = (ln1, w_in, q_norm, w_uq, kv_norm, w_ukv, w_pool, pool_scale, attn_out_norm,
              pool_out_norm, w_out, ln2, w_router, b_router, w_gate, b_gate, w_up, b_up,
              w_down, b_down)
    layers = [_prep_layer(l, *params) for l in range(ln1.shape[0])]
    return (_trunk(x_prompt, layers, final_norm), _trunk(x_sample, layers, final_norm))
```

```python
import functools

import jax
import jax.numpy as jnp
import numpy as np
from jax import lax
from jax.experimental import pallas as pl
from jax.experimental.pallas import tpu as pltpu

F32 = jnp.float32
BF16 = jnp.bfloat16
I32 = jnp.int32

D_MODEL = 1024
N_HEADS = 8
HEAD_NOPE = 64
HEAD_ROPE = 32
HEAD_V = 64
HEAD_PAD = 128
Q_LORA = 256
KV_LORA = 128
ATT_W = 512
POOL_W = 512
POOL_WINDOWS = (2, 4, 8, 16)
POOL_GW = 128
POOL_HALO = 8
ROPE_THETA = 10000.0
N_EXPERTS = 32
TOP_K = 4
SWIGLU_LIMIT = 7.0
SWIGLU_ALPHA = 1.702
NORM_EPS = 1e-5
ROUTER_PAD = 128
SUBLANES = 8

_C_Q = 0
_C_KV = _C_Q + Q_LORA
_C_KRA = _C_KV + KV_LORA
_C_KRB = _C_KRA + HEAD_PAD
_C_U = _C_KRB + HEAD_PAD
IN_W_EXT = _C_U + POOL_W

VMEM_LIMIT = 56 * 1024 * 1024


def _cparams(n_axes):
    return pltpu.CompilerParams(dimension_semantics=("arbitrary",) * n_axes,
                                vmem_limit_bytes=VMEM_LIMIT)


def _rms(x, g):
    return x * lax.rsqrt(jnp.mean(x * x, axis=-1, keepdims=True) + NORM_EPS) * g


def _proj_kernel(x_ref, ln1_ref, win_ref, qn_ref, wqa_ref, wqb_ref, kvn_ref, wk_ref, wv_ref,
                 cq_ref, sq_ref, ck_ref, sk_ref, q_ref, k_ref, v_ref, u_ref):
    h = _rms(x_ref[...], ln1_ref[...]).astype(BF16)
    proj = jnp.dot(h, win_ref[...], preferred_element_type=F32)
    u_ref[...] = proj[:, _C_U:_C_U + POOL_W]
    hq = _rms(proj[:, _C_Q:_C_Q + Q_LORA], qn_ref[...]).astype(BF16)
    qa = jnp.dot(hq, wqa_ref[...], preferred_element_type=F32)
    qb = jnp.dot(hq, wqb_ref[...], preferred_element_type=F32)
    hkv = _rms(proj[:, _C_KV:_C_KV + KV_LORA], kvn_ref[...]).astype(BF16)
    kn = jnp.dot(hkv, wk_ref[...], preferred_element_type=F32)
    vv = jnp.dot(hkv, wv_ref[...], preferred_element_type=F32)
    lane = lax.broadcasted_iota(I32, vv.shape, 1)
    v_ref[...] = jnp.where((lane & (HEAD_PAD - 1)) == HEAD_V, 1.0, vv).astype(BF16)
    k_rope = (proj[:, _C_KRA:_C_KRA + HEAD_PAD] * ck_ref[...]
              + proj[:, _C_KRB:_C_KRB + HEAD_PAD] * sk_ref[...])
    cq = cq_ref[...]
    sq = sq_ref[...]
    for hd in range(N_HEADS):
        sl = slice(hd * HEAD_PAD, (hd + 1) * HEAD_PAD)
        q_ref[:, sl] = (qa[:, sl] * cq + qb[:, sl] * sq).astype(BF16)
        k_ref[:, sl] = (kn[:, sl] + k_rope).astype(BF16)


def _proj_call(x, lw, tabs, seq_len, tm):
    T = x.shape[0]
    tiles_per_seq = seq_len // tm
    full = lambda shape: pl.BlockSpec(shape, lambda i: (0,) * len(shape))
    tab = pl.BlockSpec((tm, HEAD_PAD), lambda i: (i % tiles_per_seq, 0))
    row = lambda w: pl.BlockSpec((tm, w), lambda i: (i, 0))
    return pl.pallas_call(
        _proj_kernel,
        grid=(T // tm,),
        in_specs=[row(D_MODEL), full((1, D_MODEL)), full((D_MODEL, IN_W_EXT)),
                  full((1, Q_LORA)), full((Q_LORA, N_HEADS * HEAD_PAD)),
                  full((Q_LORA, N_HEADS * HEAD_PAD)),
                  full((1, KV_LORA)), full((KV_LORA, N_HEADS * HEAD_PAD)),
                  full((KV_LORA, N_HEADS * HEAD_PAD)), tab, tab, tab, tab],
        out_specs=[row(N_HEADS * HEAD_PAD), row(N_HEADS * HEAD_PAD), row(N_HEADS * HEAD_PAD),
                   row(POOL_W)],
        out_shape=[jax.ShapeDtypeStruct((T, N_HEADS * HEAD_PAD), BF16),
                   jax.ShapeDtypeStruct((T, N_HEADS * HEAD_PAD), BF16),
                   jax.ShapeDtypeStruct((T, N_HEADS * HEAD_PAD), BF16),
                   jax.ShapeDtypeStruct((T, POOL_W), F32)],
        compiler_params=_cparams(1),
        name="proj",
    )(x, lw["ln1"], lw["w_in"], lw["q_norm"], lw["w_qa"], lw["w_qb"], lw["kv_norm"],
      lw["w_k"], lw["w_v"], tabs["cq"], tabs["sq"], tabs["ck"], tabs["sk"])


def _attn_kernel(q_ref, k_ref, v_ref, o_ref, m_scr, acc_scr, sa_scr, *, tk, n_chunks):
    m_scr[...] = jnp.full(m_scr.shape, -jnp.inf, F32)
    acc_scr[...] = jnp.zeros(acc_scr.shape, F32)

    def head_cols(hh):
        return slice(hh * HEAD_PAD, (hh + 1) * HEAD_PAD)

    def chunk(j):
        return pl.ds(j * tk if isinstance(j, int) else pl.multiple_of(j * tk, tk), tk)

    def scores(hh, j):
        kc = k_ref[chunk(j), head_cols(hh)]
        return lax.dot_general(q_ref[:, head_cols(hh)], kc, (((1,), (1,)), ((), ())),
                               preferred_element_type=F32)

    def update(hh, s, j):
        vc = v_ref[chunk(j), head_cols(hh)]
        m_old = m_scr[hh]
        m_new = jnp.maximum(m_old, jnp.max(s, axis=-1, keepdims=True))
        alpha = jnp.exp2(m_old - m_new)
        p = jnp.exp2(s - m_new).astype(BF16)
        acc_scr[hh] = alpha * acc_scr[hh] + jnp.dot(p, vc, preferred_element_type=F32)
        m_scr[hh] = m_new

    sa_scr[...] = scores(0, 0)

    def step(j, carry):
        sb = scores(1, j)
        update(0, sa_scr[...], j)
        sa_scr[...] = scores(0, j + 1)
        update(1, sb, j)
        return carry

    lax.fori_loop(0, n_chunks - 1, step, 0)
    sb = scores(1, n_chunks - 1)
    update(0, sa_scr[...], n_chunks - 1)
    update(1, sb, n_chunks - 1)
    outs = []
    for hh in range(2):
        acc = acc_scr[hh]
        outs.append(acc[:, :HEAD_V] / acc[:, HEAD_V:HEAD_V + 1])
    o_ref[...] = jnp.concatenate(outs, axis=-1).astype(o_ref.dtype)


def _attn_call(q, k, v, batch, seq_len, tq, tk):
    T = q.shape[0]
    n_q = seq_len // tq
    kern = functools.partial(_attn_kernel, tk=tk, n_chunks=seq_len // tk)
    pair = lambda rows: pl.BlockSpec((rows, 2 * HEAD_PAD), lambda b, hp, i: (b, hp))
    return pl.pallas_call(
        kern,
        grid=(batch, N_HEADS // 2, n_q),
        in_specs=[pl.BlockSpec((tq, 2 * HEAD_PAD), lambda b, hp, i: (b * n_q + i, hp)),
                  pair(seq_len), pair(seq_len)],
        out_specs=pl.BlockSpec((tq, 2 * HEAD_V), lambda b, hp, i: (b * n_q + i, hp)),
        out_shape=jax.ShapeDtypeStruct((T, ATT_W), BF16),
        scratch_shapes=[pltpu.VMEM((2, tq, 1), F32), pltpu.VMEM((2, tq, HEAD_PAD), F32),
                        pltpu.VMEM((tq, tk), F32)],
        compiler_params=_cparams(3),
        name="attn",
    )(q, k, v)


def _mix_kernel(x_ref, o_ref, u_ref, up_ref, un_ref, an_ref, pn_ref, wpool_ref, pscale_ref,
                wout_ref, ln2_ref, wrh_ref, wrl_ref, br_ref,
                xn_ref, h2_ref, idx_ref, gate_ref, rank_ref, cnt_ref, ubuf, *, tm, seq_len):
    i = pl.program_id(0)
    tiles_per_seq = seq_len // tm
    ti = i % tiles_per_seq
    prev_ok = jnp.where(ti > 0, 1.0, 0.0).astype(F32)
    next_ok = jnp.where(ti < tiles_per_seq - 1, 1.0, 0.0).astype(F32)
    ubuf[0:POOL_HALO, :] = up_ref[...] * prev_ok
    ubuf[POOL_HALO:POOL_HALO + tm, :] = u_ref[...]
    ubuf[POOL_HALO + tm:2 * POOL_HALO + tm, :] = un_ref[...] * next_ok

    pos = ti * tm + lax.broadcasted_iota(I32, (tm, 1), 0)
    pooled = []
    for g, w in enumerate(POOL_WINDOWS):
        cols = slice(g * POOL_GW, (g + 1) * POOL_GW)
        tot = jnp.zeros((tm, POOL_GW), F32)
        for d in range(-(w // 2), w - w // 2):
            tot = tot + ubuf[POOL_HALO + d:POOL_HALO + d + tm, cols]
        lo = jnp.maximum(pos - w // 2, 0)
        hi = jnp.minimum(pos + (w - w // 2), seq_len)
        cnt = (hi - lo).astype(F32)
        pg = (tot / cnt - u_ref[:, cols]).astype(BF16)
        yg = jnp.dot(pg, wpool_ref[g], preferred_element_type=F32) * pscale_ref[:, cols]
        pooled.append(yg)
    pool = jnp.concatenate(pooled, axis=-1)
    attn = o_ref[...].astype(F32)
    mixed = jnp.concatenate([_rms(attn, an_ref[...]), _rms(pool, pn_ref[...])],
                            axis=-1).astype(BF16)
    xn = x_ref[...] + jnp.dot(mixed, wout_ref[...], preferred_element_type=F32)
    xn_ref[...] = xn
    h2 = _rms(xn, ln2_ref[...])
    hi_ = h2.astype(BF16)
    h2_ref[...] = hi_
    lo_ = (h2 - hi_.astype(F32)).astype(BF16)
    logits = (jnp.dot(hi_, wrh_ref[...], preferred_element_type=F32)
              + jnp.dot(lo_, wrh_ref[...], preferred_element_type=F32)
              + jnp.dot(hi_, wrl_ref[...], preferred_element_type=F32)) + br_ref[...]
    lane = lax.broadcasted_iota(I32, logits.shape, 1)
    vals, idxs = [], []
    for _ in range(TOP_K):
        mx = jnp.max(logits, axis=-1, keepdims=True)
        ix = jnp.min(jnp.where(logits == mx, lane, ROUTER_PAD), axis=-1, keepdims=True)
        vals.append(mx)
        idxs.append(ix)
        logits = jnp.where(lane == ix, -jnp.inf, logits)
    es = [jnp.exp(vk - vals[0]) for vk in vals]
    den = es[0] + es[1] + es[2] + es[3]
    hits = [lane == idxs[kk] for kk in range(TOP_K)]
    routed = jnp.where(hits[0] | hits[1] | hits[2] | hits[3], 1.0, 0.0)
    r_i = lax.broadcasted_iota(I32, (tm, tm), 0)
    c_i = lax.broadcasted_iota(I32, (tm, tm), 1)
    ltri = jnp.where(c_i < r_i, 1.0, 0.0).astype(BF16)
    before = jnp.dot(ltri, routed.astype(BF16), preferred_element_type=F32)
    idx_out = jnp.zeros(lane.shape, I32)
    gate_out = jnp.zeros(lane.shape, F32)
    rank_out = jnp.zeros(lane.shape, I32)
    for kk in range(TOP_K):
        rk = jnp.sum(jnp.where(hits[kk], before, 0.0), axis=-1, keepdims=True).astype(I32)
        idx_out = jnp.where(lane == kk, idxs[kk], idx_out)
        gate_out = jnp.where(lane == kk, es[kk] / den, gate_out)
        rank_out = jnp.where(lane == kk, rk, rank_out)
    idx_ref[...] = idx_out
    gate_ref[...] = gate_out
    rank_ref[...] = rank_out
    counts = jnp.sum(routed, axis=0, keepdims=True).astype(I32)
    cnt_ref[0] = jnp.broadcast_to(counts, (SUBLANES, ROUTER_PAD))


def _mix_call(x, o, u, lw, seq_len, tm):
    T = x.shape[0]
    hb = tm // POOL_HALO
    n_hblk = T // POOL_HALO
    full = lambda shape: pl.BlockSpec(shape, lambda i: (0,) * len(shape))
    row = lambda w: pl.BlockSpec((tm, w), lambda i: (i, 0))
    kern = functools.partial(_mix_kernel, tm=tm, seq_len=seq_len)
    return pl.pallas_call(
        kern,
        grid=(T // tm,),
        in_specs=[row(D_MODEL), row(ATT_W), row(POOL_W),
                  pl.BlockSpec((POOL_HALO, POOL_W), lambda i: (jnp.maximum(i * hb - 1, 0), 0)),
                  pl.BlockSpec((POOL_HALO, POOL_W),
                               lambda i: (jnp.minimum((i + 1) * hb, n_hblk - 1), 0)),
                  full((1, ATT_W)), full((1, POOL_W)), full((4, POOL_GW, POOL_GW)),
                  full((1, POOL_W)), full((ATT_W + POOL_W, D_MODEL)), full((1, D_MODEL)),
                  full((D_MODEL, ROUTER_PAD)), full((D_MODEL, ROUTER_PAD)),
                  full((1, ROUTER_PAD))],
        out_specs=[row(D_MODEL), row(D_MODEL), row(ROUTER_PAD), row(ROUTER_PAD), row(ROUTER_PAD),
                   pl.BlockSpec((1, SUBLANES, ROUTER_PAD), lambda i: (i, 0, 0))],
        out_shape=[jax.ShapeDtypeStruct((T, D_MODEL), F32),
                   jax.ShapeDtypeStruct((T, D_MODEL), BF16),
                   jax.ShapeDtypeStruct((T, ROUTER_PAD), I32),
                   jax.ShapeDtypeStruct((T, ROUTER_PAD), F32),
                   jax.ShapeDtypeStruct((T, ROUTER_PAD), I32),
                   jax.ShapeDtypeStruct((T // tm, SUBLANES, ROUTER_PAD), I32)],
        scratch_shapes=[pltpu.VMEM((tm + 2 * POOL_HALO, POOL_W), F32)],
        compiler_params=_cparams(1),
        name="mix",
    )(x, o, u, u, u, lw["attn_out_norm"], lw["pool_out_norm"], lw["w_pool"], lw["pool_scale"],
      lw["w_out"], lw["ln2"], lw["w_r_hi"], lw["w_r_lo"], lw["b_r"])


def _run_starts(cnt_tile):
    n_ch_e = ((cnt_tile + SUBLANES - 1) // SUBLANES).astype(F32)
    e_r = lax.broadcasted_iota(I32, (ROUTER_PAD, ROUTER_PAD), 0)
    e_c = lax.broadcasted_iota(I32, (ROUTER_PAD, ROUTER_PAD), 1)
    before = jnp.where(e_r < e_c, 1.0, 0.0).astype(BF16)
    starts = jnp.dot(n_ch_e.astype(BF16), before, preferred_element_type=F32)[0:1, :]
    return starts * float(SUBLANES)


def _local_rows(idx, rank, run_start, tm):
    lane = lax.broadcasted_iota(I32, (tm, ROUTER_PAD), 1)
    rows = []
    for kk in range(TOP_K):
        e_k = jnp.sum(jnp.where(lane == kk, idx, 0), axis=-1, keepdims=True)
        start_k = jnp.sum(jnp.where(lane == e_k, run_start, 0.0), axis=-1, keepdims=True)
        rank_k = jnp.sum(jnp.where(lane == kk, rank, 0), axis=-1, keepdims=True)
        rows.append(start_k.astype(I32) + rank_k)
    return rows


def _localsort_kernel(h_ref, idx_ref, rank_ref, cnt_ref, xl_ref, *, tm, lrows):
    rows = _local_rows(idx_ref[...], rank_ref[...], _run_starts(cnt_ref[0]), tm)
    lane = lax.broadcasted_iota(I32, (tm, ROUTER_PAD), 1)
    rows_pad = jnp.full((tm, ROUTER_PAD), -1.0, F32)
    for kk in range(TOP_K):
        rows_pad = jnp.where(lane == kk, rows[kk].astype(F32), rows_pad)
    rows_t = rows_pad.T
    r_iota = lax.broadcasted_iota(I32, (lrows, tm), 0).astype(F32)
    sel = jnp.zeros((lrows, tm), F32)
    for kk in range(TOP_K):
        sel = jnp.where(r_iota == rows_t[kk:kk + 1, :], 1.0, sel)
    xl_ref[...] = jnp.dot(sel.astype(BF16), h_ref[...], preferred_element_type=F32)


def _localsort_call(h2, idx_pad, rank_pad, tile_cnt, tm):
    T = h2.shape[0]
    n_tiles = T // tm
    lrows = _local_buffer_rows(tm)
    row = lambda w: pl.BlockSpec((tm, w), lambda i: (i, 0))
    return pl.pallas_call(
        functools.partial(_localsort_kernel, tm=tm, lrows=lrows),
        grid=(n_tiles,),
        in_specs=[row(D_MODEL), row(ROUTER_PAD), row(ROUTER_PAD),
                  pl.BlockSpec((1, SUBLANES, ROUTER_PAD), lambda i: (i, 0, 0))],
        out_specs=pl.BlockSpec((lrows, D_MODEL), lambda i: (i, 0)),
        out_shape=jax.ShapeDtypeStruct((n_tiles * lrows, D_MODEL), F32),
        compiler_params=_cparams(1),
        name="localsort",
    )(h2, idx_pad, rank_pad, tile_cnt)


def _ffn_kernel(be_ref, nu_ref, src0_ref, srcn_ref, xl_ref, wg_ref, bg_ref, wu_ref, bu_ref,
                wd_ref, bd_ref, y_ref, xbuf, sem, *, bm, n_blocks):
    del be_ref
    i = pl.program_id(0)
    nu = nu_ref[0]
    slot = i % 2
    n_ch = bm // SUBLANES

    def chunk_copy(src_ref, s, c):
        src_row = pl.multiple_of(src_ref[0, 0, c], SUBLANES)
        return pltpu.make_async_copy(xl_ref.at[pl.ds(src_row, SUBLANES)],
                                     xbuf.at[s, pl.ds(c * SUBLANES, SUBLANES)], sem.at[s])

    def block_wait(s):
        pltpu.make_async_copy(xl_ref.at[pl.ds(0, bm)], xbuf.at[s], sem.at[s]).wait()

    @pl.when(i == 0)
    def _():
        for c in range(n_ch):
            chunk_copy(src0_ref, 0, c).start()

    @pl.when(i < nu)
    def _():
        for c in range(n_ch):
            chunk_copy(srcn_ref, 1 - slot, c).start()
        block_wait(slot)
        x = xbuf[slot].astype(BF16)
        gt = jnp.minimum(jnp.dot(x, wg_ref[0], preferred_element_type=F32) + bg_ref[0],
                         SWIGLU_LIMIT)
        up = jnp.clip(jnp.dot(x, wu_ref[0], preferred_element_type=F32) + bu_ref[0],
                      -SWIGLU_LIMIT, SWIGLU_LIMIT)
        act = gt * jax.nn.sigmoid(SWIGLU_ALPHA * gt) * (up + 1.0)
        y_ref[...] = jnp.dot(act.astype(BF16), wd_ref[0], preferred_element_type=F32) + bd_ref[0]

    @pl.when(i >= nu)
    def _():
        y_ref[...] = jnp.zeros(y_ref.shape, y_ref.dtype)

    @pl.when(i == n_blocks - 1)
    def _():
        block_wait(nu % 2)


def _ffn_call(xl, chunk_src, block_expert, n_used, lw, bm):
    n_blocks = chunk_src.shape[0]
    wmap = lambda i, be, nu: (be[i], 0, 0)
    stab = lambda imap: pl.BlockSpec((1, 1, ROUTER_PAD), imap, memory_space=pltpu.SMEM)
    grid_spec = pltpu.PrefetchScalarGridSpec(
        num_scalar_prefetch=2,
        grid=(n_blocks,),
        in_specs=[stab(lambda i, be, nu: (0, 0, 0)),
                  stab(lambda i, be, nu: (jnp.minimum(i + 1, n_blocks - 1), 0, 0)),
                  pl.BlockSpec(memory_space=pl.ANY),
                  pl.BlockSpec((1, D_MODEL, D_MODEL), wmap), pl.BlockSpec((1, 1, D_MODEL), wmap),
                  pl.BlockSpec((1, D_MODEL, D_MODEL), wmap), pl.BlockSpec((1, 1, D_MODEL), wmap),
                  pl.BlockSpec((1, D_MODEL, D_MODEL), wmap), pl.BlockSpec((1, 1, D_MODEL), wmap)],
        out_specs=pl.BlockSpec((bm, D_MODEL), lambda i, be, nu: (i, 0)),
        scratch_shapes=[pltpu.VMEM((2, bm, D_MODEL), F32), pltpu.SemaphoreType.DMA((2,))],
    )
    return pl.pallas_call(
        functools.partial(_ffn_kernel, bm=bm, n_blocks=n_blocks),
        grid_spec=grid_spec,
        out_shape=jax.ShapeDtypeStruct((n_blocks * bm, D_MODEL), F32),
        compiler_params=_cparams(1),
        name="ffn",
    )(block_expert, n_used, chunk_src, chunk_src, xl, lw["w_gate"], lw["b_gate"], lw["w_up"],
      lw["b_up"], lw["w_down"], lw["b_down"])


def _combine_kernel(gs_ref, cn_ref, x_ref, idx_ref, rank_ref, gate_ref, cnt_ref, fn_ref, y_ref,
                    out_ref, ylocal, sem, *, tm, lrows, final):
    @pl.when(pl.program_id(0) == 0)
    def _():
        ylocal[...] = jnp.zeros(ylocal.shape, ylocal.dtype)

    def chunk_copy(src_row, dst_row):
        return pltpu.make_async_copy(y_ref.at[pl.ds(src_row, SUBLANES)],
                                     ylocal.at[pl.ds(dst_row, SUBLANES)], sem)

    def per_expert(e, off):
        n_ch = (cn_ref[0, 0, e] + SUBLANES - 1) // SUBLANES
        src = gs_ref[0, 0, e]

        def per_chunk(c, carry):
            chunk_copy(pl.multiple_of(src + c * SUBLANES, SUBLANES),
                       pl.multiple_of(off + c * SUBLANES, SUBLANES)).start()
            return carry

        lax.fori_loop(0, n_ch, per_chunk, 0)
        return off + n_ch * SUBLANES

    used = lax.fori_loop(0, N_EXPERTS, per_expert, 0)

    def wait_chunk(c, carry):
        chunk_copy(0, 0).wait()
        return carry

    lax.fori_loop(0, used // SUBLANES, wait_chunk, 0)

    rows = _local_rows(idx_ref[...], rank_ref[...], _run_starts(cnt_ref[0]), tm)
    lane = lax.broadcasted_iota(I32, (tm, ROUTER_PAD), 1)
    gates = gate_ref[...]
    r_iota = lax.broadcasted_iota(I32, (tm, lrows), 1)
    gmat = jnp.zeros((tm, lrows), F32)
    for kk in range(TOP_K):
        g_k = jnp.sum(jnp.where(lane == kk, gates, 0.0), axis=-1, keepdims=True)
        gmat = jnp.where(r_iota == rows[kk], g_k, gmat)
    moe = jnp.dot(gmat.astype(BF16), ylocal[...].astype(BF16), preferred_element_type=F32)
    acc = x_ref[...] + moe
    if final:
        acc = _rms(acc, fn_ref[...])
    out_ref[...] = acc


def _combine_call(x, idx_pad, rank_pad, gate_pad, tile_cnt, gstart, cnt, y, final_norm, tm,
                  final):
    T = x.shape[0]
    n_tiles = T // tm
    lrows = _local_buffer_rows(tm)
    tab = lambda a: jnp.pad(a, ((0, 0), (0, ROUTER_PAD - N_EXPERTS))).reshape(
        n_tiles, 1, ROUTER_PAD)
    kern = functools.partial(_combine_kernel, tm=tm, lrows=lrows, final=final)
    stab = pl.BlockSpec((1, 1, ROUTER_PAD), lambda i: (i, 0, 0), memory_space=pltpu.SMEM)
    row = lambda w: pl.BlockSpec((tm, w), lambda i: (i, 0))
    return pl.pallas_call(
        kern,
        grid=(n_tiles,),
        in_specs=[stab, stab, row(D_MODEL), row(ROUTER_PAD), row(ROUTER_PAD), row(ROUTER_PAD),
                  pl.BlockSpec((1, SUBLANES, ROUTER_PAD), lambda i: (i, 0, 0)),
                  pl.BlockSpec((1, D_MODEL), lambda i: (0, 0)),
                  pl.BlockSpec(memory_space=pl.ANY)],
        out_specs=row(D_MODEL),
        out_shape=jax.ShapeDtypeStruct((T, D_MODEL), F32),
        scratch_shapes=[pltpu.VMEM((lrows, D_MODEL), F32), pltpu.SemaphoreType.DMA],
        compiler_params=_cparams(1),
        name="combine",
    )(tab(gstart), tab(cnt), x, idx_pad, rank_pad, gate_pad, tile_cnt, final_norm, y)


def _local_buffer_rows(tm):
    return tm * TOP_K + N_EXPERTS * SUBLANES


def _routing_tables(tile_cnt, tm, bm):
    n_tiles = tile_cnt.shape[0]
    lrows = _local_buffer_rows(tm)
    ch_blk = bm // SUBLANES
    cnt = tile_cnt[:, 0, :N_EXPERTS]
    n_ch = (cnt + SUBLANES - 1) // SUBLANES
    reg = jnp.sum(n_ch, axis=0)
    reg_pad = (reg + ch_blk - 1) // ch_blk * ch_blk
    cend = jnp.cumsum(reg_pad)
    cstart = cend - reg_pad
    before_tile = jnp.cumsum(n_ch, axis=0) - n_ch
    gstart = ((cstart[None, :] + before_tile) * SUBLANES).astype(I32)
    lstart = jnp.cumsum(n_ch, axis=1) - n_ch
    max_rows = n_tiles * (tm * TOP_K + N_EXPERTS * (SUBLANES - 1))
    n_blocks = (max_rows + bm - 1) // bm + N_EXPERTS
    blk_first = jnp.arange(n_blocks, dtype=I32) * ch_blk
    block_expert = jnp.minimum(jnp.sum(cend[None, :] <= blk_first[:, None], axis=1),
                               N_EXPERTS - 1).astype(I32)
    n_used = (cend[-1] // ch_blk).astype(I32).reshape(1)
    g = jnp.arange(n_blocks * ch_blk, dtype=I32)
    e_g = block_expert[g // ch_blk]
    o = g - cstart[e_g]
    ends = (before_tile + n_ch).T[e_g]
    tile_g = jnp.minimum(jnp.sum(ends <= o[:, None], axis=1), n_tiles - 1)
    flat = tile_g * N_EXPERTS + e_g
    src = (tile_g * (lrows // SUBLANES) + lstart.reshape(-1)[flat]
           + o - before_tile.reshape(-1)[flat])
    src = jnp.where(o < reg[e_g], src, lrows // SUBLANES - 1) * SUBLANES
    chunk_src = jnp.pad(src.reshape(n_blocks, ch_blk).astype(I32),
                        ((0, 0), (0, ROUTER_PAD - ch_blk))).reshape(n_blocks, 1, ROUTER_PAD)
    return gstart, cnt, chunk_src, block_expert, n_used


def _rope_tables(seq_len):
    inv = 1.0 / (ROPE_THETA ** (jnp.arange(0, HEAD_ROPE, 2, dtype=F32) / HEAD_ROPE))
    ang = jnp.arange(seq_len, dtype=F32)[:, None] * inv[None, :]
    cos, sin = jnp.cos(ang), jnp.sin(ang)
    scale = np.float32(np.log2(np.e) / np.sqrt(HEAD_NOPE + HEAD_ROPE))
    z = jnp.zeros((seq_len, HEAD_PAD - HEAD_NOPE - HEAD_ROPE), F32)
    ones = jnp.ones((seq_len, HEAD_NOPE), F32)
    zeros = jnp.zeros((seq_len, HEAD_NOPE), F32)
    ck = jnp.concatenate([zeros, cos, cos, z], axis=1)
    sk = jnp.concatenate([zeros, sin, sin, z], axis=1)
    cq = jnp.concatenate([ones, cos, cos, z], axis=1) * scale
    sq = sk * scale
    return {"cq": cq, "sq": sq, "ck": ck, "sk": sk}


def _pad_heads(w_nope, w_r1, w_r2):
    lead = w_nope.shape[:-2]
    z = jnp.zeros(lead + (N_HEADS, HEAD_PAD - HEAD_NOPE - HEAD_ROPE), w_nope.dtype)
    return jnp.concatenate([w_nope, w_r1, w_r2, z], axis=-1).reshape(lead + (N_HEADS * HEAD_PAD,))


def _prep_layer(l, ln1, w_in, q_norm, w_uq, kv_norm, w_ukv, w_pool, pool_scale, attn_out_norm,
                pool_out_norm, w_out, ln2, w_router, b_router, w_gate, b_gate, w_up, b_up,
                w_down, b_down):
    half = HEAD_ROPE // 2
    wi = w_in[l]
    kr = wi[:, Q_LORA + KV_LORA:Q_LORA + KV_LORA + HEAD_ROPE]
    kr1, kr2 = kr[:, :half], kr[:, half:]
    zn = jnp.zeros((D_MODEL, HEAD_NOPE), F32)
    zp = jnp.zeros((D_MODEL, HEAD_PAD - HEAD_NOPE - HEAD_ROPE), F32)
    kra = jnp.concatenate([zn, kr1, kr2, zp], axis=1)
    krb = jnp.concatenate([zn, -kr2, kr1, zp], axis=1)
    w_in_ext = jnp.concatenate(
        [wi[:, :Q_LORA + KV_LORA], kra, krb, wi[:, Q_LORA + KV_LORA + HEAD_ROPE:]], axis=1)
    wq = w_uq[l].reshape(Q_LORA, N_HEADS, HEAD_NOPE + HEAD_ROPE)
    qn_, q1, q2 = wq[..., :HEAD_NOPE], wq[..., HEAD_NOPE:HEAD_NOPE + half], wq[..., HEAD_NOPE + half:]
    w_qa = _pad_heads(qn_, q1, q2)
    w_qb = _pad_heads(jnp.zeros_like(qn_), -q2, q1)
    wkv = w_ukv[l].reshape(KV_LORA, N_HEADS, HEAD_NOPE + HEAD_V)
    kz = jnp.zeros((KV_LORA, N_HEADS, half), F32)
    w_k = _pad_heads(wkv[..., :HEAD_NOPE], kz, kz)
    w_v = jnp.pad(wkv[..., HEAD_NOPE:], ((0, 0), (0, 0), (0, HEAD_PAD - HEAD_V))).reshape(
        KV_LORA, N_HEADS * HEAD_PAD)
    wr = jnp.pad(w_router[l], ((0, 0), (0, ROUTER_PAD - N_EXPERTS)))
    wr_hi = wr.astype(BF16)
    wr_lo = (wr - wr_hi.astype(F32)).astype(BF16)
    br = jnp.concatenate([b_router[l].astype(F32),
                          jnp.full((ROUTER_PAD - N_EXPERTS,), -jnp.inf, F32)]).reshape(1, ROUTER_PAD)
    return {
        "ln1": ln1[l].reshape(1, -1), "w_in": w_in_ext.astype(BF16),
        "q_norm": q_norm[l].reshape(1, -1), "w_qa": w_qa.astype(BF16), "w_qb": w_qb.astype(BF16),
        "kv_norm": kv_norm[l].reshape(1, -1), "w_k": w_k.astype(BF16), "w_v": w_v.astype(BF16),
        "w_pool": w_pool[l].astype(BF16), "pool_scale": pool_scale[l].reshape(1, -1),
        "attn_out_norm": attn_out_norm[l].reshape(1, -1),
        "pool_out_norm": pool_out_norm[l].reshape(1, -1),
        "w_out": w_out[l].astype(BF16), "ln2": ln2[l].reshape(1, -1),
        "w_r_hi": wr_hi, "w_r_lo": wr_lo, "b_r": br,
        "w_gate": w_gate[l].astype(BF16), "b_gate": b_gate[l].reshape(N_EXPERTS, 1, -1),
        "w_up": w_up[l].astype(BF16), "b_up": b_up[l].reshape(N_EXPERTS, 1, -1),
        "w_down": w_down[l].astype(BF16), "b_down": b_down[l].reshape(N_EXPERTS, 1, -1),
    }


def _tiles(seq_len):
    tm = min(512, seq_len)
    tq = min(512, seq_len)
    tk = min(2048, seq_len)
    bm = 512
    return tm, tq, tk, bm


def _trunk(x3, layers, final_norm):
    B, S, _ = x3.shape
    T = B * S
    tm, tq, tk, bm = _tiles(S)
    tabs = _rope_tables(S)
    x = x3.reshape(T, D_MODEL)
    fn = final_norm.reshape(1, -1)
    for li, lw in enumerate(layers):
        q, k, v, u = _proj_call(x, lw, tabs, S, tm)
        o = _attn_call(q, k, v, B, S, tq, tk)
        x, h2, idx_pad, gate_pad, rank_pad, tile_cnt = _mix_call(x, o, u, lw, S, tm)
        gstart, cnt, chunk_src, block_expert, n_used = _routing_tables(tile_cnt, tm, bm)
        xl = _localsort_call(h2, idx_pad, rank_pad, tile_cnt, tm)
        y = _ffn_call(xl, chunk_src, block_expert, n_used, lw, bm)
        x = _combine_call(x, idx_pad, rank_pad, gate_pad, tile_cnt, gstart, cnt, y, fn, tm,
                          final=(li == len(layers) - 1))
    return x.reshape(B, S, D_MODEL)


def kernel(x_prompt, x_sample, ln1, w_in, q_norm, w_uq, kv_norm, w_ukv, w_pool, pool_scale,
           attn_out_norm, pool_out_norm, w_out, ln2, w_router, b_router, w_gate, b_gate, w_up,
           b_up, w_down, b_down, final_norm):
    params = (ln1, w_in, q_norm, w_uq, kv_norm, w_ukv, w_pool, pool_scale, attn_out_norm,
              pool_out_norm, w_out, ln2, w_router, b_router, w_gate, b_gate, w_up, b_up,
              w_down, b_down)
    layers = [_prep_layer(l, *params) for l in range(ln1.shape[0])]
    return (_trunk(x_prompt, layers, final_norm), _trunk(x_sample, layers, final_norm))
```

```python
import functools

import jax
import jax.numpy as jnp
import numpy as np
from jax import lax
from jax.experimental import pallas as pl
from jax.experimental.pallas import tpu as pltpu

F32 = jnp.float32
BF16 = jnp.bfloat16
I32 = jnp.int32

D_MODEL = 1024
N_HEADS = 8
HEAD_NOPE = 64
HEAD_ROPE = 32
HEAD_V = 64
HEAD_PAD = 128
Q_LORA = 256
KV_LORA = 128
ATT_W = 512
POOL_W = 512
POOL_WINDOWS = (2, 4, 8, 16)
POOL_GW = 128
POOL_HALO = 8
ROPE_THETA = 10000.0
N_EXPERTS = 32
TOP_K = 4
SWIGLU_LIMIT = 7.0
SWIGLU_ALPHA = 1.702
NORM_EPS = 1e-5
ROUTER_PAD = 128
SUBLANES = 8

_C_Q = 0
_C_KV = _C_Q + Q_LORA
_C_KRA = _C_KV + KV_LORA
_C_KRB = _C_KRA + HEAD_PAD
_C_U = _C_KRB + HEAD_PAD
IN_W_EXT = _C_U + POOL_W

VMEM_LIMIT = 56 * 1024 * 1024


def _cparams(n_axes):
    return pltpu.CompilerParams(dimension_semantics=("arbitrary",) * n_axes,
                                vmem_limit_bytes=VMEM_LIMIT)


def _rms(x, g):
    return x * lax.rsqrt(jnp.mean(x * x, axis=-1, keepdims=True) + NORM_EPS) * g


def _proj_kernel(x_ref, ln1_ref, win_ref, qn_ref, wqa_ref, wqb_ref, kvn_ref, wk_ref, wv_ref,
                 cq_ref, sq_ref, ck_ref, sk_ref, q_ref, k_ref, v_ref, u_ref):
    h = _rms(x_ref[...], ln1_ref[...]).astype(BF16)
    proj = jnp.dot(h, win_ref[...], preferred_element_type=F32)
    u_ref[...] = proj[:, _C_U:_C_U + POOL_W]
    hq = _rms(proj[:, _C_Q:_C_Q + Q_LORA], qn_ref[...]).astype(BF16)
    qa = jnp.dot(hq, wqa_ref[...], preferred_element_type=F32)
    qb = jnp.dot(hq, wqb_ref[...], preferred_element_type=F32)
    hkv = _rms(proj[:, _C_KV:_C_KV + KV_LORA], kvn_ref[...]).astype(BF16)
    kn = jnp.dot(hkv, wk_ref[...], preferred_element_type=F32)
    vv = jnp.dot(hkv, wv_ref[...], preferred_element_type=F32)
    lane = lax.broadcasted_iota(I32, vv.shape, 1)
    v_ref[...] = jnp.where((lane & (HEAD_PAD - 1)) == HEAD_V, 1.0, vv).astype(BF16)
    k_rope = (proj[:, _C_KRA:_C_KRA + HEAD_PAD] * ck_ref[...]
              + proj[:, _C_KRB:_C_KRB + HEAD_PAD] * sk_ref[...])
    cq = cq_ref[...]
    sq = sq_ref[...]
    for hd in range(N_HEADS):
        sl = slice(hd * HEAD_PAD, (hd + 1) * HEAD_PAD)
        q_ref[:, sl] = (qa[:, sl] * cq + qb[:, sl] * sq).astype(BF16)
        k_ref[:, sl] = (kn[:, sl] + k_rope).astype(BF16)


def _proj_call(x, lw, tabs, seq_len, tm):
    T = x.shape[0]
    tiles_per_seq = seq_len // tm
    full = lambda shape: pl.BlockSpec(shape, lambda i: (0,) * len(shape))
    tab = pl.BlockSpec((tm, HEAD_PAD), lambda i: (i % tiles_per_seq, 0))
    row = lambda w: pl.BlockSpec((tm, w), lambda i: (i, 0))
    return pl.pallas_call(
        _proj_kernel,
        grid=(T // tm,),
        in_specs=[row(D_MODEL), full((1, D_MODEL)), full((D_MODEL, IN_W_EXT)),
                  full((1, Q_LORA)), full((Q_LORA, N_HEADS * HEAD_PAD)),
                  full((Q_LORA, N_HEADS * HEAD_PAD)),
                  full((1, KV_LORA)), full((KV_LORA, N_HEADS * HEAD_PAD)),
                  full((KV_LORA, N_HEADS * HEAD_PAD)), tab, tab, tab, tab],
        out_specs=[row(N_HEADS * HEAD_PAD), row(N_HEADS * HEAD_PAD), row(N_HEADS * HEAD_PAD),
                   row(POOL_W)],
        out_shape=[jax.ShapeDtypeStruct((T, N_HEADS * HEAD_PAD), BF16),
                   jax.ShapeDtypeStruct((T, N_HEADS * HEAD_PAD), BF16),
                   jax.ShapeDtypeStruct((T, N_HEADS * HEAD_PAD), BF16),
                   jax.ShapeDtypeStruct((T, POOL_W), F32)],
        compiler_params=_cparams(1),
        name="proj",
    )(x, lw["ln1"], lw["w_in"], lw["q_norm"], lw["w_qa"], lw["w_qb"], lw["kv_norm"],
      lw["w_k"], lw["w_v"], tabs["cq"], tabs["sq"], tabs["ck"], tabs["sk"])


def _attn_kernel(q_ref, k_ref, v_ref, o_ref, m_scr, acc_scr, sa_scr, *, tq, tk, n_sub, n_chunks):
    m_scr[...] = jnp.full(m_scr.shape, -jnp.inf, F32)
    acc_scr[...] = jnp.zeros(acc_scr.shape, F32)
    n_steps = n_sub * n_chunks

    def head_cols(hh):
        return slice(hh * HEAD_PAD, (hh + 1) * HEAD_PAD)

    def block(i, size):
        return pl.ds(i * size if isinstance(i, int) else pl.multiple_of(i * size, size), size)

    def split(n):
        return (n // n_chunks, n % n_chunks) if isinstance(n, int) else (
            lax.div(n, n_chunks), lax.rem(n, n_chunks))

    def scores(hh, n):
        sub, j = split(n)
        kc = k_ref[block(j, tk), head_cols(hh)]
        return lax.dot_general(q_ref[block(sub, tq), head_cols(hh)], kc,
                               (((1,), (1,)), ((), ())),
                               preferred_element_type=F32)

    def update(hh, s, n):
        sub, j = split(n)
        vc = v_ref[block(j, tk), head_cols(hh)]
        m_old = m_scr[sub, hh]
        m_new = jnp.maximum(m_old, jnp.max(s, axis=-1, keepdims=True))
        alpha = jnp.exp2(m_old - m_new)
        p = jnp.exp2(s - m_new).astype(BF16)
        acc_scr[sub, hh] = alpha * acc_scr[sub, hh] + jnp.dot(p, vc, preferred_element_type=F32)
        m_scr[sub, hh] = m_new

    sa_scr[...] = scores(0, 0)

    def step(n, carry):
        sb = scores(1, n)
        update(0, sa_scr[...], n)
        sa_scr[...] = scores(0, n + 1)
        update(1, sb, n)
        return carry

    lax.fori_loop(0, n_steps - 1, step, 0)
    sb = scores(1, n_steps - 1)
    update(0, sa_scr[...], n_steps - 1)
    update(1, sb, n_steps - 1)
    for sub in range(n_sub):
        outs = []
        for hh in range(2):
            acc = acc_scr[sub, hh]
            outs.append(acc[:, :HEAD_V] / acc[:, HEAD_V:HEAD_V + 1])
        o_ref[sub * tq:(sub + 1) * tq, :] = jnp.concatenate(outs, axis=-1).astype(o_ref.dtype)


def _attn_call(q, k, v, batch, seq_len, tq, tk, n_sub):
    T = q.shape[0]
    rows = tq * n_sub
    n_q = seq_len // rows
    kern = functools.partial(_attn_kernel, tq=tq, tk=tk, n_sub=n_sub, n_chunks=seq_len // tk)
    pair = lambda r: pl.BlockSpec((r, 2 * HEAD_PAD), lambda b, hp, i: (b, hp))
    return pl.pallas_call(
        kern,
        grid=(batch, N_HEADS // 2, n_q),
        in_specs=[pl.BlockSpec((rows, 2 * HEAD_PAD), lambda b, hp, i: (b * n_q + i, hp)),
                  pair(seq_len), pair(seq_len)],
        out_specs=pl.BlockSpec((rows, 2 * HEAD_V), lambda b, hp, i: (b * n_q + i, hp)),
        out_shape=jax.ShapeDtypeStruct((T, ATT_W), BF16),
        scratch_shapes=[pltpu.VMEM((n_sub, 2, tq, 1), F32),
                        pltpu.VMEM((n_sub, 2, tq, HEAD_PAD), F32),
                        pltpu.VMEM((tq, tk), F32)],
        compiler_params=_cparams(3),
        name="attn",
    )(q, k, v)


def _mix_kernel(x_ref, o_ref, u_ref, up_ref, un_ref, an_ref, pn_ref, wpool_ref, pscale_ref,
                wout_ref, ln2_ref, wrh_ref, wrl_ref, br_ref,
                xn_ref, h2_ref, idx_ref, gate_ref, rank_ref, cnt_ref, ubuf, *, tm, seq_len):
    i = pl.program_id(0)
    tiles_per_seq = seq_len // tm
    ti = i % tiles_per_seq
    prev_ok = jnp.where(ti > 0, 1.0, 0.0).astype(F32)
    next_ok = jnp.where(ti < tiles_per_seq - 1, 1.0, 0.0).astype(F32)
    ubuf[0:POOL_HALO, :] = up_ref[...] * prev_ok
    ubuf[POOL_HALO:POOL_HALO + tm, :] = u_ref[...]
    ubuf[POOL_HALO + tm:2 * POOL_HALO + tm, :] = un_ref[...] * next_ok

    pos = ti * tm + lax.broadcasted_iota(I32, (tm, 1), 0)
    pooled = []
    for g, w in enumerate(POOL_WINDOWS):
        cols = slice(g * POOL_GW, (g + 1) * POOL_GW)
        tot = jnp.zeros((tm, POOL_GW), F32)
        for d in range(-(w // 2), w - w // 2):
            tot = tot + ubuf[POOL_HALO + d:POOL_HALO + d + tm, cols]
        lo = jnp.maximum(pos - w // 2, 0)
        hi = jnp.minimum(pos + (w - w // 2), seq_len)
        cnt = (hi - lo).astype(F32)
        pg = (tot / cnt - u_ref[:, cols]).astype(BF16)
        yg = jnp.dot(pg, wpool_ref[g], preferred_element_type=F32) * pscale_ref[:, cols]
        pooled.append(yg)
    pool = jnp.concatenate(pooled, axis=-1)
    attn = o_ref[...].astype(F32)
    mixed = jnp.concatenate([_rms(attn, an_ref[...]), _rms(pool, pn_ref[...])],
                            axis=-1).astype(BF16)
    xn = x_ref[...] + jnp.dot(mixed, wout_ref[...], preferred_element_type=F32)
    xn_ref[...] = xn
    h2 = _rms(xn, ln2_ref[...])
    hi_ = h2.astype(BF16)
    h2_ref[...] = hi_
    lo_ = (h2 - hi_.astype(F32)).astype(BF16)
    logits = (jnp.dot(hi_, wrh_ref[...], preferred_element_type=F32)
              + jnp.dot(lo_, wrh_ref[...], preferred_element_type=F32)
              + jnp.dot(hi_, wrl_ref[...], preferred_element_type=F32)) + br_ref[...]
    lane = lax.broadcasted_iota(I32, logits.shape, 1)
    vals, idxs = [], []
    for _ in range(TOP_K):
        mx = jnp.max(logits, axis=-1, keepdims=True)
        ix = jnp.min(jnp.where(logits == mx, lane, ROUTER_PAD), axis=-1, keepdims=True)
        vals.append(mx)
        idxs.append(ix)
        logits = jnp.where(lane == ix, -jnp.inf, logits)
    es = [jnp.exp(vk - vals[0]) for vk in vals]
    den = es[0] + es[1] + es[2] + es[3]
    hits = [lane == idxs[kk] for kk in range(TOP_K)]
    routed = jnp.where(hits[0] | hits[1] | hits[2] | hits[3], 1.0, 0.0)
    r_i = lax.broadcasted_iota(I32, (tm, tm), 0)
    c_i = lax.broadcasted_iota(I32, (tm, tm), 1)
    ltri = jnp.where(c_i < r_i, 1.0, 0.0).astype(BF16)
    before = jnp.dot(ltri, routed.astype(BF16), preferred_element_type=F32)
    idx_out = jnp.zeros(lane.shape, I32)
    gate_out = jnp.zeros(lane.shape, F32)
    rank_out = jnp.zeros(lane.shape, I32)
    for kk in range(TOP_K):
        rk = jnp.sum(jnp.where(hits[kk], before, 0.0), axis=-1, keepdims=True).astype(I32)
        idx_out = jnp.where(lane == kk, idxs[kk], idx_out)
        gate_out = jnp.where(lane == kk, es[kk] / den, gate_out)
        rank_out = jnp.where(lane == kk, rk, rank_out)
    idx_ref[...] = idx_out
    gate_ref[...] = gate_out
    rank_ref[...] = rank_out
    counts = jnp.sum(routed, axis=0, keepdims=True).astype(I32)
    cnt_ref[0] = jnp.broadcast_to(counts, (SUBLANES, ROUTER_PAD))


def _mix_call(x, o, u, lw, seq_len, tm):
    T = x.shape[0]
    hb = tm // POOL_HALO
    n_hblk = T // POOL_HALO
    full = lambda shape: pl.BlockSpec(shape, lambda i: (0,) * len(shape))
    row = lambda w: pl.BlockSpec((tm, w), lambda i: (i, 0))
    kern = functools.partial(_mix_kernel, tm=tm, seq_len=seq_len)
    return pl.pallas_call(
        kern,
        grid=(T // tm,),
        in_specs=[row(D_MODEL), row(ATT_W), row(POOL_W),
                  pl.BlockSpec((POOL_HALO, POOL_W), lambda i: (jnp.maximum(i * hb - 1, 0), 0)),
                  pl.BlockSpec((POOL_HALO, POOL_W),
                               lambda i: (jnp.minimum((i + 1) * hb, n_hblk - 1), 0)),
                  full((1, ATT_W)), full((1, POOL_W)), full((4, POOL_GW, POOL_GW)),
                  full((1, POOL_W)), full((ATT_W + POOL_W, D_MODEL)), full((1, D_MODEL)),
                  full((D_MODEL, ROUTER_PAD)), full((D_MODEL, ROUTER_PAD)),
                  full((1, ROUTER_PAD))],
        out_specs=[row(D_MODEL), row(D_MODEL), row(ROUTER_PAD), row(ROUTER_PAD), row(ROUTER_PAD),
                   pl.BlockSpec((1, SUBLANES, ROUTER_PAD), lambda i: (i, 0, 0))],
        out_shape=[jax.ShapeDtypeStruct((T, D_MODEL), F32),
                   jax.ShapeDtypeStruct((T, D_MODEL), BF16),
                   jax.ShapeDtypeStruct((T, ROUTER_PAD), I32),
                   jax.ShapeDtypeStruct((T, ROUTER_PAD), F32),
                   jax.ShapeDtypeStruct((T, ROUTER_PAD), I32),
                   jax.ShapeDtypeStruct((T // tm, SUBLANES, ROUTER_PAD), I32)],
        scratch_shapes=[pltpu.VMEM((tm + 2 * POOL_HALO, POOL_W), F32)],
        compiler_params=_cparams(1),
        name="mix",
    )(x, o, u, u, u, lw["attn_out_norm"], lw["pool_out_norm"], lw["w_pool"], lw["pool_scale"],
      lw["w_out"], lw["ln2"], lw["w_r_hi"], lw["w_r_lo"], lw["b_r"])


def _run_starts(cnt_tile):
    n_ch_e = ((cnt_tile + SUBLANES - 1) // SUBLANES).astype(F32)
    e_r = lax.broadcasted_iota(I32, (ROUTER_PAD, ROUTER_PAD), 0)
    e_c = lax.broadcasted_iota(I32, (ROUTER_PAD, ROUTER_PAD), 1)
    before = jnp.where(e_r < e_c, 1.0, 0.0).astype(BF16)
    starts = jnp.dot(n_ch_e.astype(BF16), before, preferred_element_type=F32)[0:1, :]
    return starts * float(SUBLANES)


def _local_rows(idx, rank, run_start, tm):
    lane = lax.broadcasted_iota(I32, (tm, ROUTER_PAD), 1)
    rows = []
    for kk in range(TOP_K):
        e_k = jnp.sum(jnp.where(lane == kk, idx, 0), axis=-1, keepdims=True)
        start_k = jnp.sum(jnp.where(lane == e_k, run_start, 0.0), axis=-1, keepdims=True)
        rank_k = jnp.sum(jnp.where(lane == kk, rank, 0), axis=-1, keepdims=True)
        rows.append(start_k.astype(I32) + rank_k)
    return rows


def _localsort_kernel(h_ref, idx_ref, rank_ref, cnt_ref, xl_ref, *, tm, lrows):
    rows = _local_rows(idx_ref[...], rank_ref[...], _run_starts(cnt_ref[0]), tm)
    lane = lax.broadcasted_iota(I32, (tm, ROUTER_PAD), 1)
    rows_pad = jnp.full((tm, ROUTER_PAD), -1.0, F32)
    for kk in range(TOP_K):
        rows_pad = jnp.where(lane == kk, rows[kk].astype(F32), rows_pad)
    rows_t = rows_pad.T
    r_iota = lax.broadcasted_iota(I32, (lrows, tm), 0).astype(F32)
    sel = jnp.zeros((lrows, tm), F32)
    for kk in range(TOP_K):
        sel = jnp.where(r_iota == rows_t[kk:kk + 1, :], 1.0, sel)
    xl_ref[...] = jnp.dot(sel.astype(BF16), h_ref[...], preferred_element_type=F32)


def _localsort_call(h2, idx_pad, rank_pad, tile_cnt, tm):
    T = h2.shape[0]
    n_tiles = T // tm
    lrows = _local_buffer_rows(tm)
    row = lambda w: pl.BlockSpec((tm, w), lambda i: (i, 0))
    return pl.pallas_call(
        functools.partial(_localsort_kernel, tm=tm, lrows=lrows),
        grid=(n_tiles,),
        in_specs=[row(D_MODEL), row(ROUTER_PAD), row(ROUTER_PAD),
                  pl.BlockSpec((1, SUBLANES, ROUTER_PAD), lambda i: (i, 0, 0))],
        out_specs=pl.BlockSpec((lrows, D_MODEL), lambda i: (i, 0)),
        out_shape=jax.ShapeDtypeStruct((n_tiles * lrows, D_MODEL), F32),
        compiler_params=_cparams(1),
        name="localsort",
    )(h2, idx_pad, rank_pad, tile_cnt)


def _ffn_kernel(be_ref, nu_ref, src0_ref, srcn_ref, xl_ref, wg_ref, bg_ref, wu_ref, bu_ref,
                wd_ref, bd_ref, y_ref, xbuf, sem, *, bm, n_blocks):
    del be_ref
    i = pl.program_id(0)
    nu = nu_ref[0]
    slot = i % 2
    n_ch = bm // SUBLANES

    def chunk_copy(src_ref, s, c):
        src_row = pl.multiple_of(src_ref[0, 0, c], SUBLANES)
        return pltpu.make_async_copy(xl_ref.at[pl.ds(src_row, SUBLANES)],
                                     xbuf.at[s, pl.ds(c * SUBLANES, SUBLANES)], sem.at[s])

    def block_wait(s):
        pltpu.make_async_copy(xl_ref.at[pl.ds(0, bm)], xbuf.at[s], sem.at[s]).wait()

    @pl.when(i == 0)
    def _():
        for c in range(n_ch):
            chunk_copy(src0_ref, 0, c).start()

    @pl.when(i < nu)
    def _():
        for c in range(n_ch):
            chunk_copy(srcn_ref, 1 - slot, c).start()
        block_wait(slot)
        x = xbuf[slot].astype(BF16)
        gt = jnp.minimum(jnp.dot(x, wg_ref[0], preferred_element_type=F32) + bg_ref[0],
                         SWIGLU_LIMIT)
        up = jnp.clip(jnp.dot(x, wu_ref[0], preferred_element_type=F32) + bu_ref[0],
                      -SWIGLU_LIMIT, SWIGLU_LIMIT)
        act = gt * jax.nn.sigmoid(SWIGLU_ALPHA * gt) * (up + 1.0)
        y_ref[...] = jnp.dot(act.astype(BF16), wd_ref[0], preferred_element_type=F32) + bd_ref[0]

    @pl.when(i >= nu)
    def _():
        y_ref[...] = jnp.zeros(y_ref.shape, y_ref.dtype)

    @pl.when(i == n_blocks - 1)
    def _():
        block_wait(nu % 2)


def _ffn_call(xl, chunk_src, block_expert, n_used, lw, bm):
    n_blocks = chunk_src.shape[0]
    wmap = lambda i, be, nu: (be[i], 0, 0)
    stab = lambda imap: pl.BlockSpec((1, 1, ROUTER_PAD), imap, memory_space=pltpu.SMEM)
    grid_spec = pltpu.PrefetchScalarGridSpec(
        num_scalar_prefetch=2,
        grid=(n_blocks,),
        in_specs=[stab(lambda i, be, nu: (0, 0, 0)),
                  stab(lambda i, be, nu: (jnp.minimum(i + 1, n_blocks - 1), 0, 0)),
                  pl.BlockSpec(memory_space=pl.ANY),
                  pl.BlockSpec((1, D_MODEL, D_MODEL), wmap), pl.BlockSpec((1, 1, D_MODEL), wmap),
                  pl.BlockSpec((1, D_MODEL, D_MODEL), wmap), pl.BlockSpec((1, 1, D_MODEL), wmap),
                  pl.BlockSpec((1, D_MODEL, D_MODEL), wmap), pl.BlockSpec((1, 1, D_MODEL), wmap)],
        out_specs=pl.BlockSpec((bm, D_MODEL), lambda i, be, nu: (i, 0)),
        scratch_shapes=[pltpu.VMEM((2, bm, D_MODEL), F32), pltpu.SemaphoreType.DMA((2,))],
    )
    return pl.pallas_call(
        functools.partial(_ffn_kernel, bm=bm, n_blocks=n_blocks),
        grid_spec=grid_spec,
        out_shape=jax.ShapeDtypeStruct((n_blocks * bm, D_MODEL), F32),
        compiler_params=_cparams(1),
        name="ffn",
    )(block_expert, n_used, chunk_src, chunk_src, xl, lw["w_gate"], lw["b_gate"], lw["w_up"],
      lw["b_up"], lw["w_down"], lw["b_down"])


def _combine_kernel(gs_ref, cn_ref, x_ref, idx_ref, rank_ref, gate_ref, cnt_ref, fn_ref, y_ref,
                    out_ref, ylocal, sem, *, tm, lrows, final):
    @pl.when(pl.program_id(0) == 0)
    def _():
        ylocal[...] = jnp.zeros(ylocal.shape, ylocal.dtype)

    def chunk_copy(src_row, dst_row):
        return pltpu.make_async_copy(y_ref.at[pl.ds(src_row, SUBLANES)],
                                     ylocal.at[pl.ds(dst_row, SUBLANES)], sem)

    def per_expert(e, off):
        n_ch = (cn_ref[0, 0, e] + SUBLANES - 1) // SUBLANES
        src = gs_ref[0, 0, e]

        def per_chunk(c, carry):
            chunk_copy(pl.multiple_of(src + c * SUBLANES, SUBLANES),
                       pl.multiple_of(off + c * SUBLANES, SUBLANES)).start()
            return carry

        lax.fori_loop(0, n_ch, per_chunk, 0)
        return off + n_ch * SUBLANES

    used = lax.fori_loop(0, N_EXPERTS, per_expert, 0)

    def wait_chunk(c, carry):
        chunk_copy(0, 0).wait()
        return carry

    lax.fori_loop(0, used // SUBLANES, wait_chunk, 0)

    rows = _local_rows(idx_ref[...], rank_ref[...], _run_starts(cnt_ref[0]), tm)
    lane = lax.broadcasted_iota(I32, (tm, ROUTER_PAD), 1)
    gates = gate_ref[...]
    r_iota = lax.broadcasted_iota(I32, (tm, lrows), 1)
    gmat = jnp.zeros((tm, lrows), F32)
    for kk in range(TOP_K):
        g_k = jnp.sum(jnp.where(lane == kk, gates, 0.0), axis=-1, keepdims=True)
        gmat = jnp.where(r_iota == rows[kk], g_k, gmat)
    moe = jnp.dot(gmat.astype(BF16), ylocal[...].astype(BF16), preferred_element_type=F32)
    acc = x_ref[...] + moe
    if final:
        acc = _rms(acc, fn_ref[...])
    out_ref[...] = acc


def _combine_call(x, idx_pad, rank_pad, gate_pad, tile_cnt, gstart, cnt, y, final_norm, tm,
                  final):
    T = x.shape[0]
    n_tiles = T // tm
    lrows = _local_buffer_rows(tm)
    tab = lambda a: jnp.pad(a, ((0, 0), (0, ROUTER_PAD - N_EXPERTS))).reshape(
        n_tiles, 1, ROUTER_PAD)
    kern = functools.partial(_combine_kernel, tm=tm, lrows=lrows, final=final)
    stab = pl.BlockSpec((1, 1, ROUTER_PAD), lambda i: (i, 0, 0), memory_space=pltpu.SMEM)
    row = lambda w: pl.BlockSpec((tm, w), lambda i: (i, 0))
    return pl.pallas_call(
        kern,
        grid=(n_tiles,),
        in_specs=[stab, stab, row(D_MODEL), row(ROUTER_PAD), row(ROUTER_PAD), row(ROUTER_PAD),
                  pl.BlockSpec((1, SUBLANES, ROUTER_PAD), lambda i: (i, 0, 0)),
                  pl.BlockSpec((1, D_MODEL), lambda i: (0, 0)),
                  pl.BlockSpec(memory_space=pl.ANY)],
        out_specs=row(D_MODEL),
        out_shape=jax.ShapeDtypeStruct((T, D_MODEL), F32),
        scratch_shapes=[pltpu.VMEM((lrows, D_MODEL), F32), pltpu.SemaphoreType.DMA],
        compiler_params=_cparams(1),
        name="combine",
    )(tab(gstart), tab(cnt), x, idx_pad, rank_pad, gate_pad, tile_cnt, final_norm, y)


def _local_buffer_rows(tm):
    return tm * TOP_K + N_EXPERTS * SUBLANES


def _routing_tables(tile_cnt, tm, bm):
    n_tiles = tile_cnt.shape[0]
    lrows = _local_buffer_rows(tm)
    ch_blk = bm // SUBLANES
    cnt = tile_cnt[:, 0, :N_EXPERTS]
    n_ch = (cnt + SUBLANES - 1) // SUBLANES
    reg = jnp.sum(n_ch, axis=0)
    reg_pad = (reg + ch_blk - 1) // ch_blk * ch_blk
    cend = jnp.cumsum(reg_pad)
    cstart = cend - reg_pad
    before_tile = jnp.cumsum(n_ch, axis=0) - n_ch
    gstart = ((cstart[None, :] + before_tile) * SUBLANES).astype(I32)
    lstart = jnp.cumsum(n_ch, axis=1) - n_ch
    max_rows = n_tiles * (tm * TOP_K + N_EXPERTS * (SUBLANES - 1))
    n_blocks = (max_rows + bm - 1) // bm + N_EXPERTS
    blk_first = jnp.arange(n_blocks, dtype=I32) * ch_blk
    block_expert = jnp.minimum(jnp.sum(cend[None, :] <= blk_first[:, None], axis=1),
                               N_EXPERTS - 1).astype(I32)
    n_used = (cend[-1] // ch_blk).astype(I32).reshape(1)
    mine = block_expert[:, None] == jnp.arange(N_EXPERTS, dtype=I32)[None, :]
    pick = lambda tab: jnp.sum(jnp.where(mine[:, None, :], tab[None, :, :], 0), axis=-1)
    cstart_b = jnp.sum(jnp.where(mine, cstart[None, :], 0), axis=-1)
    reg_b = jnp.sum(jnp.where(mine, reg[None, :], 0), axis=-1)
    before_b = pick(before_tile)
    ends_b = pick(before_tile + n_ch)
    lstart_b = pick(lstart)
    o = (blk_first - cstart_b)[:, None] + jnp.arange(ch_blk, dtype=I32)[None, :]
    tile_c = jnp.minimum(jnp.sum(ends_b[:, None, :] <= o[:, :, None], axis=-1), n_tiles - 1)
    at_tile = tile_c[:, :, None] == jnp.arange(n_tiles, dtype=I32)[None, None, :]
    local = jnp.sum(jnp.where(at_tile, (lstart_b - before_b)[:, None, :], 0), axis=-1)
    src = tile_c * (lrows // SUBLANES) + local + o
    src = jnp.where(o < reg_b[:, None], src, lrows // SUBLANES - 1) * SUBLANES
    chunk_src = jnp.pad(src.astype(I32),
                        ((0, 0), (0, ROUTER_PAD - ch_blk))).reshape(n_blocks, 1, ROUTER_PAD)
    return gstart, cnt, chunk_src, block_expert, n_used


def _rope_tables(seq_len):
    inv = 1.0 / (ROPE_THETA ** (jnp.arange(0, HEAD_ROPE, 2, dtype=F32) / HEAD_ROPE))
    ang = jnp.arange(seq_len, dtype=F32)[:, None] * inv[None, :]
    cos, sin = jnp.cos(ang), jnp.sin(ang)
    scale = np.float32(np.log2(np.e) / np.sqrt(HEAD_NOPE + HEAD_ROPE))
    z = jnp.zeros((seq_len, HEAD_PAD - HEAD_NOPE - HEAD_ROPE), F32)
    ones = jnp.ones((seq_len, HEAD_NOPE), F32)
    zeros = jnp.zeros((seq_len, HEAD_NOPE), F32)
    ck = jnp.concatenate([zeros, cos, cos, z], axis=1)
    sk = jnp.concatenate([zeros, sin, sin, z], axis=1)
    cq = jnp.concatenate([ones, cos, cos, z], axis=1) * scale
    sq = sk * scale
    return {"cq": cq, "sq": sq, "ck": ck, "sk": sk}


def _pad_heads(w_nope, w_r1, w_r2):
    lead = w_nope.shape[:-2]
    z = jnp.zeros(lead + (N_HEADS, HEAD_PAD - HEAD_NOPE - HEAD_ROPE), w_nope.dtype)
    return jnp.concatenate([w_nope, w_r1, w_r2, z], axis=-1).reshape(lead + (N_HEADS * HEAD_PAD,))


def _prep_layer(l, ln1, w_in, q_norm, w_uq, kv_norm, w_ukv, w_pool, pool_scale, attn_out_norm,
                pool_out_norm, w_out, ln2, w_router, b_router, w_gate, b_gate, w_up, b_up,
                w_down, b_down):
    half = HEAD_ROPE // 2
    wi = w_in[l]
    kr = wi[:, Q_LORA + KV_LORA:Q_LORA + KV_LORA + HEAD_ROPE]
    kr1, kr2 = kr[:, :half], kr[:, half:]
    zn = jnp.zeros((D_MODEL, HEAD_NOPE), F32)
    zp = jnp.zeros((D_MODEL, HEAD_PAD - HEAD_NOPE - HEAD_ROPE), F32)
    kra = jnp.concatenate([zn, kr1, kr2, zp], axis=1)
    krb = jnp.concatenate([zn, -kr2, kr1, zp], axis=1)
    w_in_ext = jnp.concatenate(
        [wi[:, :Q_LORA + KV_LORA], kra, krb, wi[:, Q_LORA + KV_LORA + HEAD_ROPE:]], axis=1)
    wq = w_uq[l].reshape(Q_LORA, N_HEADS, HEAD_NOPE + HEAD_ROPE)
    qn_, q1, q2 = wq[..., :HEAD_NOPE], wq[..., HEAD_NOPE:HEAD_NOPE + half], wq[..., HEAD_NOPE + half:]
    w_qa = _pad_heads(qn_, q1, q2)
    w_qb = _pad_heads(jnp.zeros_like(qn_), -q2, q1)
    wkv = w_ukv[l].reshape(KV_LORA, N_HEADS, HEAD_NOPE + HEAD_V)
    kz = jnp.zeros((KV_LORA, N_HEADS, half), F32)
    w_k = _pad_heads(wkv[..., :HEAD_NOPE], kz, kz)
    w_v = jnp.pad(wkv[..., HEAD_NOPE:], ((0, 0), (0, 0), (0, HEAD_PAD - HEAD_V))).reshape(
        KV_LORA, N_HEADS * HEAD_PAD)
    wr = jnp.pad(w_router[l], ((0, 0), (0, ROUTER_PAD - N_EXPERTS)))
    wr_hi = wr.astype(BF16)
    wr_lo = (wr - wr_hi.astype(F32)).astype(BF16)
    br = jnp.concatenate([b_router[l].astype(F32),
                          jnp.full((ROUTER_PAD - N_EXPERTS,), -jnp.inf, F32)]).reshape(1, ROUTER_PAD)
    return {
        "ln1": ln1[l].reshape(1, -1), "w_in": w_in_ext.astype(BF16),
        "q_norm": q_norm[l].reshape(1, -1), "w_qa": w_qa.astype(BF16), "w_qb": w_qb.astype(BF16),
        "kv_norm": kv_norm[l].reshape(1, -1), "w_k": w_k.astype(BF16), "w_v": w_v.astype(BF16),
        "w_pool": w_pool[l].astype(BF16), "pool_scale": pool_scale[l].reshape(1, -1),
        "attn_out_norm": attn_out_norm[l].reshape(1, -1),
        "pool_out_norm": pool_out_norm[l].reshape(1, -1),
        "w_out": w_out[l].astype(BF16), "ln2": ln2[l].reshape(1, -1),
        "w_r_hi": wr_hi, "w_r_lo": wr_lo, "b_r": br,
        "w_gate": w_gate[l].astype(BF16), "b_gate": b_gate[l].reshape(N_EXPERTS, 1, -1),
        "w_up": w_up[l].astype(BF16), "b_up": b_up[l].reshape(N_EXPERTS, 1, -1),
        "w_down": w_down[l].astype(BF16), "b_down": b_down[l].reshape(N_EXPERTS, 1, -1),
    }


def _tiles(seq_len):
    tm = min(512, seq_len)
    tq = min(512, seq_len)
    n_sub = min(2048, seq_len) // tq
    tk = min(2048, seq_len)
    bm = 512
    return tm, tq, n_sub, tk, bm


def _trunk(x3, layers, final_norm):
    B, S, _ = x3.shape
    T = B * S
    tm, tq, n_sub, tk, bm = _tiles(S)
    tabs = _rope_tables(S)
    x = x3.reshape(T, D_MODEL)
    fn = final_norm.reshape(1, -1)
    for li, lw in enumerate(layers):
        q, k, v, u = _proj_call(x, lw, tabs, S, tm)
        o = _attn_call(q, k, v, B, S, tq, tk, n_sub)
        x, h2, idx_pad, gate_pad, rank_pad, tile_cnt = _mix_call(x, o, u, lw, S, tm)
        gstart, cnt, chunk_src, block_expert, n_used = _routing_tables(tile_cnt, tm, bm)
        xl = _localsort_call(h2, idx_pad, rank_pad, tile_cnt, tm)
        y = _ffn_call(xl, chunk_src, block_expert, n_used, lw, bm)
        x = _combine_call(x, idx_pad, rank_pad, gate_pad, tile_cnt, gstart, cnt, y, fn, tm,
                          final=(li == len(layers) - 1))
    return x.reshape(B, S, D_MODEL)


def kernel(x_prompt, x_sample, ln1, w_in, q_norm, w_uq, kv_norm, w_ukv, w_pool, pool_scale,
           attn_out_norm, pool_out_norm, w_out, ln2, w_router, b_router, w_gate, b_gate, w_up,
           b_up, w_down, b_down, final_norm):
    params = (ln1, w_in, q_norm, w_uq, kv_norm, w_ukv, w_pool, pool_scale, attn_out_norm,
              pool_out_norm, w_out, ln2, w_router, b_router, w_gate, b_gate, w_up, b_up,
              w_down, b_down)
    layers = [_prep_layer(l, *params) for l in range(ln1.shape[0])]
    return (_trunk(x_prompt, layers, final_norm), _trunk(x_sample, layers, final_norm))
```

```python
import functools

import jax
import jax.numpy as jnp
import numpy as np
from jax import lax
from jax.experimental import pallas as pl
from jax.experimental.pallas import tpu as pltpu

F32 = jnp.float32
BF16 = jnp.bfloat16
I32 = jnp.int32

D_MODEL = 1024
N_HEADS = 8
HEAD_NOPE = 64
HEAD_ROPE = 32
HEAD_V = 64
HEAD_PAD = 128
Q_LORA = 256
KV_LORA = 128
ATT_W = 512
POOL_W = 512
POOL_WINDOWS = (2, 4, 8, 16)
POOL_GW = 128
POOL_HALO = 8
ROPE_THETA = 10000.0
N_EXPERTS = 32
TOP_K = 4
SWIGLU_LIMIT = 7.0
SWIGLU_ALPHA = 1.702
NORM_EPS = 1e-5
ROUTER_PAD = 128
SUBLANES = 8

_C_Q = 0
_C_KV = _C_Q + Q_LORA
_C_KRA = _C_KV + KV_LORA
_C_KRB = _C_KRA + HEAD_PAD
_C_U = _C_KRB + HEAD_PAD
IN_W_EXT = _C_U + POOL_W

VMEM_LIMIT = 56 * 1024 * 1024


def _cparams(n_axes):
    return pltpu.CompilerParams(dimension_semantics=("arbitrary",) * n_axes,
                                vmem_limit_bytes=VMEM_LIMIT)


def _rms(x, g):
    return x * lax.rsqrt(jnp.mean(x * x, axis=-1, keepdims=True) + NORM_EPS) * g


def _proj_kernel(x_ref, ln1_ref, win_ref, qn_ref, wqa_ref, wqb_ref, kvn_ref, wk_ref, wv_ref,
                 cq_ref, sq_ref, ck_ref, sk_ref, q_ref, k_ref, v_ref, u_ref):
    h = _rms(x_ref[...], ln1_ref[...]).astype(BF16)
    proj = jnp.dot(h, win_ref[...], preferred_element_type=F32)
    u_ref[...] = proj[:, _C_U:_C_U + POOL_W]
    hq = _rms(proj[:, _C_Q:_C_Q + Q_LORA], qn_ref[...]).astype(BF16)
    qa = jnp.dot(hq, wqa_ref[...], preferred_element_type=F32)
    qb = jnp.dot(hq, wqb_ref[...], preferred_element_type=F32)
    hkv = _rms(proj[:, _C_KV:_C_KV + KV_LORA], kvn_ref[...]).astype(BF16)
    kn = jnp.dot(hkv, wk_ref[...], preferred_element_type=F32)
    vv = jnp.dot(hkv, wv_ref[...], preferred_element_type=F32)
    lane = lax.broadcasted_iota(I32, vv.shape, 1)
    v_ref[...] = jnp.where((lane & (HEAD_PAD - 1)) == HEAD_V, 1.0, vv).astype(BF16)
    k_rope = (proj[:, _C_KRA:_C_KRA + HEAD_PAD] * ck_ref[...]
              + proj[:, _C_KRB:_C_KRB + HEAD_PAD] * sk_ref[...])
    cq = cq_ref[...]
    sq = sq_ref[...]
    for hd in range(N_HEADS):
        sl = slice(hd * HEAD_PAD, (hd + 1) * HEAD_PAD)
        q_ref[:, sl] = (qa[:, sl] * cq + qb[:, sl] * sq).astype(BF16)
        k_ref[:, sl] = (kn[:, sl] + k_rope).astype(BF16)


def _proj_call(x, lw, tabs, seq_len, tm):
    T = x.shape[0]
    tiles_per_seq = seq_len // tm
    full = lambda shape: pl.BlockSpec(shape, lambda i: (0,) * len(shape))
    tab = pl.BlockSpec((tm, HEAD_PAD), lambda i: (i % tiles_per_seq, 0))
    row = lambda w: pl.BlockSpec((tm, w), lambda i: (i, 0))
    return pl.pallas_call(
        _proj_kernel,
        grid=(T // tm,),
        in_specs=[row(D_MODEL), full((1, D_MODEL)), full((D_MODEL, IN_W_EXT)),
                  full((1, Q_LORA)), full((Q_LORA, N_HEADS * HEAD_PAD)),
                  full((Q_LORA, N_HEADS * HEAD_PAD)),
                  full((1, KV_LORA)), full((KV_LORA, N_HEADS * HEAD_PAD)),
                  full((KV_LORA, N_HEADS * HEAD_PAD)), tab, tab, tab, tab],
        out_specs=[row(N_HEADS * HEAD_PAD), row(N_HEADS * HEAD_PAD), row(N_HEADS * HEAD_PAD),
                   row(POOL_W)],
        out_shape=[jax.ShapeDtypeStruct((T, N_HEADS * HEAD_PAD), BF16),
                   jax.ShapeDtypeStruct((T, N_HEADS * HEAD_PAD), BF16),
                   jax.ShapeDtypeStruct((T, N_HEADS * HEAD_PAD), BF16),
                   jax.ShapeDtypeStruct((T, POOL_W), F32)],
        compiler_params=_cparams(1),
        name="proj",
    )(x, lw["ln1"], lw["w_in"], lw["q_norm"], lw["w_qa"], lw["w_qb"], lw["kv_norm"],
      lw["w_k"], lw["w_v"], tabs["cq"], tabs["sq"], tabs["ck"], tabs["sk"])


def _attn_kernel(q_ref, k_ref, v_ref, o_ref, m_scr, acc_scr, sa_scr, *, tq, tk, n_sub, n_chunks):
    m_scr[...] = jnp.full(m_scr.shape, -jnp.inf, F32)
    acc_scr[...] = jnp.zeros(acc_scr.shape, F32)
    n_steps = n_sub * n_chunks

    def head_cols(hh):
        return slice(hh * HEAD_PAD, (hh + 1) * HEAD_PAD)

    def block(i, size):
        return pl.ds(i * size if isinstance(i, int) else pl.multiple_of(i * size, size), size)

    def split(n):
        return (n // n_chunks, n % n_chunks) if isinstance(n, int) else (
            lax.div(n, n_chunks), lax.rem(n, n_chunks))

    def scores(hh, n):
        sub, j = split(n)
        kc = k_ref[block(j, tk), head_cols(hh)]
        return lax.dot_general(q_ref[block(sub, tq), head_cols(hh)], kc,
                               (((1,), (1,)), ((), ())),
                               preferred_element_type=F32)

    def update(hh, s, n):
        sub, j = split(n)
        vc = v_ref[block(j, tk), head_cols(hh)]
        m_old = m_scr[sub, hh]
        m_new = jnp.maximum(m_old, jnp.max(s, axis=-1, keepdims=True))
        alpha = jnp.exp2(m_old - m_new)
        p = jnp.exp2(s - m_new).astype(BF16)
        acc_scr[sub, hh] = alpha * acc_scr[sub, hh] + jnp.dot(p, vc, preferred_element_type=F32)
        m_scr[sub, hh] = m_new

    sa_scr[...] = scores(0, 0)

    def step(n, carry):
        sb = scores(1, n)
        update(0, sa_scr[...], n)
        sa_scr[...] = scores(0, n + 1)
        update(1, sb, n)
        return carry

    lax.fori_loop(0, n_steps - 1, step, 0, unroll=2)
    sb = scores(1, n_steps - 1)
    update(0, sa_scr[...], n_steps - 1)
    update(1, sb, n_steps - 1)
    for sub in range(n_sub):
        outs = []
        for hh in range(2):
            acc = acc_scr[sub, hh]
            outs.append(acc[:, :HEAD_V] / acc[:, HEAD_V:HEAD_V + 1])
        o_ref[sub * tq:(sub + 1) * tq, :] = jnp.concatenate(outs, axis=-1).astype(o_ref.dtype)


def _attn_call(q, k, v, batch, seq_len, tq, tk, n_sub):
    T = q.shape[0]
    rows = tq * n_sub
    n_q = seq_len // rows
    kern = functools.partial(_attn_kernel, tq=tq, tk=tk, n_sub=n_sub, n_chunks=seq_len // tk)
    pair = lambda r: pl.BlockSpec((r, 2 * HEAD_PAD), lambda b, hp, i: (b, hp),
                                  pipeline_mode=pl.Buffered(1))
    return pl.pallas_call(
        kern,
        grid=(batch, N_HEADS // 2, n_q),
        in_specs=[pl.BlockSpec((rows, 2 * HEAD_PAD), lambda b, hp, i: (b * n_q + i, hp)),
                  pair(seq_len), pair(seq_len)],
        out_specs=pl.BlockSpec((rows, 2 * HEAD_V), lambda b, hp, i: (b * n_q + i, hp)),
        out_shape=jax.ShapeDtypeStruct((T, ATT_W), BF16),
        scratch_shapes=[pltpu.VMEM((n_sub, 2, tq, 1), F32),
                        pltpu.VMEM((n_sub, 2, tq, HEAD_PAD), F32),
                        pltpu.VMEM((tq, tk), F32)],
        compiler_params=_cparams(3),
        name="attn",
    )(q, k, v)


def _mix_kernel(x_ref, o_ref, u_ref, up_ref, un_ref, an_ref, pn_ref, wpool_ref, pscale_ref,
                wout_ref, ln2_ref, wrh_ref, wrl_ref, br_ref,
                xn_ref, h2_ref, idx_ref, gate_ref, rank_ref, cnt_ref, ubuf, *, tm, seq_len):
    i = pl.program_id(0)
    tiles_per_seq = seq_len // tm
    ti = i % tiles_per_seq
    prev_ok = jnp.where(ti > 0, 1.0, 0.0).astype(F32)
    next_ok = jnp.where(ti < tiles_per_seq - 1, 1.0, 0.0).astype(F32)
    ubuf[0:POOL_HALO, :] = up_ref[...] * prev_ok
    ubuf[POOL_HALO:POOL_HALO + tm, :] = u_ref[...]
    ubuf[POOL_HALO + tm:2 * POOL_HALO + tm, :] = un_ref[...] * next_ok

    pos = ti * tm + lax.broadcasted_iota(I32, (tm, 1), 0)
    pooled = []
    for g, w in enumerate(POOL_WINDOWS):
        cols = slice(g * POOL_GW, (g + 1) * POOL_GW)
        tot = jnp.zeros((tm, POOL_GW), F32)
        for d in range(-(w // 2), w - w // 2):
            tot = tot + ubuf[POOL_HALO + d:POOL_HALO + d + tm, cols]
        lo = jnp.maximum(pos - w // 2, 0)
        hi = jnp.minimum(pos + (w - w // 2), seq_len)
        cnt = (hi - lo).astype(F32)
        pg = (tot / cnt - u_ref[:, cols]).astype(BF16)
        yg = jnp.dot(pg, wpool_ref[g], preferred_element_type=F32) * pscale_ref[:, cols]
        pooled.append(yg)
    pool = jnp.concatenate(pooled, axis=-1)
    attn = o_ref[...].astype(F32)
    mixed = jnp.concatenate([_rms(attn, an_ref[...]), _rms(pool, pn_ref[...])],
                            axis=-1).astype(BF16)
    xn = x_ref[...] + jnp.dot(mixed, wout_ref[...], preferred_element_type=F32)
    xn_ref[...] = xn
    h2 = _rms(xn, ln2_ref[...])
    hi_ = h2.astype(BF16)
    h2_ref[...] = hi_
    lo_ = (h2 - hi_.astype(F32)).astype(BF16)
    logits = (jnp.dot(hi_, wrh_ref[...], preferred_element_type=F32)
              + jnp.dot(lo_, wrh_ref[...], preferred_element_type=F32)
              + jnp.dot(hi_, wrl_ref[...], preferred_element_type=F32)) + br_ref[...]
    lane = lax.broadcasted_iota(I32, logits.shape, 1)
    vals, idxs = [], []
    for _ in range(TOP_K):
        mx = jnp.max(logits, axis=-1, keepdims=True)
        ix = jnp.min(jnp.where(logits == mx, lane, ROUTER_PAD), axis=-1, keepdims=True)
        vals.append(mx)
        idxs.append(ix)
        logits = jnp.where(lane == ix, -jnp.inf, logits)
    es = [jnp.exp(vk - vals[0]) for vk in vals]
    den = es[0] + es[1] + es[2] + es[3]
    hits = [lane == idxs[kk] for kk in range(TOP_K)]
    routed = jnp.where(hits[0] | hits[1] | hits[2] | hits[3], 1.0, 0.0)
    r_i = lax.broadcasted_iota(I32, (tm, tm), 0)
    c_i = lax.broadcasted_iota(I32, (tm, tm), 1)
    ltri = jnp.where(c_i < r_i, 1.0, 0.0).astype(BF16)
    before = jnp.dot(ltri, routed.astype(BF16), preferred_element_type=F32)
    idx_out = jnp.zeros(lane.shape, I32)
    gate_out = jnp.zeros(lane.shape, F32)
    rank_out = jnp.zeros(lane.shape, I32)
    for kk in range(TOP_K):
        rk = jnp.sum(jnp.where(hits[kk], before, 0.0), axis=-1, keepdims=True).astype(I32)
        idx_out = jnp.where(lane == kk, idxs[kk], idx_out)
        gate_out = jnp.where(lane == kk, es[kk] / den, gate_out)
        rank_out = jnp.where(lane == kk, rk, rank_out)
    idx_ref[...] = idx_out
    gate_ref[...] = gate_out
    rank_ref[...] = rank_out
    counts = jnp.sum(routed, axis=0, keepdims=True).astype(I32)
    cnt_ref[0] = jnp.broadcast_to(counts, (SUBLANES, ROUTER_PAD))


def _mix_call(x, o, u, lw, seq_len, tm):
    T = x.shape[0]
    hb = tm // POOL_HALO
    n_hblk = T // POOL_HALO
    full = lambda shape: pl.BlockSpec(shape, lambda i: (0,) * len(shape))
    row = lambda w: pl.BlockSpec((tm, w), lambda i: (i, 0))
    kern = functools.partial(_mix_kernel, tm=tm, seq_len=seq_len)
    return pl.pallas_call(
        kern,
        grid=(T // tm,),
        in_specs=[row(D_MODEL), row(ATT_W), row(POOL_W),
                  pl.BlockSpec((POOL_HALO, POOL_W), lambda i: (jnp.maximum(i * hb - 1, 0), 0)),
                  pl.BlockSpec((POOL_HALO, POOL_W),
                               lambda i: (jnp.minimum((i + 1) * hb, n_hblk - 1), 0)),
                  full((1, ATT_W)), full((1, POOL_W)), full((4, POOL_GW, POOL_GW)),
                  full((1, POOL_W)), full((ATT_W + POOL_W, D_MODEL)), full((1, D_MODEL)),
                  full((D_MODEL, ROUTER_PAD)), full((D_MODEL, ROUTER_PAD)),
                  full((1, ROUTER_PAD))],
        out_specs=[row(D_MODEL), row(D_MODEL), row(ROUTER_PAD), row(ROUTER_PAD), row(ROUTER_PAD),
                   pl.BlockSpec((1, SUBLANES, ROUTER_PAD), lambda i: (i, 0, 0))],
        out_shape=[jax.ShapeDtypeStruct((T, D_MODEL), F32),
                   jax.ShapeDtypeStruct((T, D_MODEL), BF16),
                   jax.ShapeDtypeStruct((T, ROUTER_PAD), I32),
                   jax.ShapeDtypeStruct((T, ROUTER_PAD), F32),
                   jax.ShapeDtypeStruct((T, ROUTER_PAD), I32),
                   jax.ShapeDtypeStruct((T // tm, SUBLANES, ROUTER_PAD), I32)],
        scratch_shapes=[pltpu.VMEM((tm + 2 * POOL_HALO, POOL_W), F32)],
        compiler_params=_cparams(1),
        name="mix",
    )(x, o, u, u, u, lw["attn_out_norm"], lw["pool_out_norm"], lw["w_pool"], lw["pool_scale"],
      lw["w_out"], lw["ln2"], lw["w_r_hi"], lw["w_r_lo"], lw["b_r"])


def _run_starts(cnt_tile):
    n_ch_e = ((cnt_tile + SUBLANES - 1) // SUBLANES).astype(F32)
    e_r = lax.broadcasted_iota(I32, (ROUTER_PAD, ROUTER_PAD), 0)
    e_c = lax.broadcasted_iota(I32, (ROUTER_PAD, ROUTER_PAD), 1)
    before = jnp.where(e_r < e_c, 1.0, 0.0).astype(BF16)
    starts = jnp.dot(n_ch_e.astype(BF16), before, preferred_element_type=F32)[0:1, :]
    return starts * float(SUBLANES)


def _local_rows(idx, rank, run_start, tm):
    lane = lax.broadcasted_iota(I32, (tm, ROUTER_PAD), 1)
    rows = []
    for kk in range(TOP_K):
        e_k = jnp.sum(jnp.where(lane == kk, idx, 0), axis=-1, keepdims=True)
        start_k = jnp.sum(jnp.where(lane == e_k, run_start, 0.0), axis=-1, keepdims=True)
        rank_k = jnp.sum(jnp.where(lane == kk, rank, 0), axis=-1, keepdims=True)
        rows.append(start_k.astype(I32) + rank_k)
    return rows


def _localsort_kernel(h_ref, idx_ref, rank_ref, cnt_ref, xl_ref, *, tm, lrows):
    rows = _local_rows(idx_ref[...], rank_ref[...], _run_starts(cnt_ref[0]), tm)
    lane = lax.broadcasted_iota(I32, (tm, ROUTER_PAD), 1)
    rows_pad = jnp.full((tm, ROUTER_PAD), -1.0, F32)
    for kk in range(TOP_K):
        rows_pad = jnp.where(lane == kk, rows[kk].astype(F32), rows_pad)
    rows_t = rows_pad.T
    r_iota = lax.broadcasted_iota(I32, (lrows, tm), 0).astype(F32)
    sel = jnp.zeros((lrows, tm), F32)
    for kk in range(TOP_K):
        sel = jnp.where(r_iota == rows_t[kk:kk + 1, :], 1.0, sel)
    xl_ref[...] = jnp.dot(sel.astype(BF16), h_ref[...], preferred_element_type=F32)


def _localsort_call(h2, idx_pad, rank_pad, tile_cnt, tm):
    T = h2.shape[0]
    n_tiles = T // tm
    lrows = _local_buffer_rows(tm)
    row = lambda w: pl.BlockSpec((tm, w), lambda i: (i, 0))
    return pl.pallas_call(
        functools.partial(_localsort_kernel, tm=tm, lrows=lrows),
        grid=(n_tiles,),
        in_specs=[row(D_MODEL), row(ROUTER_PAD), row(ROUTER_PAD),
                  pl.BlockSpec((1, SUBLANES, ROUTER_PAD), lambda i: (i, 0, 0))],
        out_specs=pl.BlockSpec((lrows, D_MODEL), lambda i: (i, 0)),
        out_shape=jax.ShapeDtypeStruct((n_tiles * lrows, D_MODEL), F32),
        compiler_params=_cparams(1),
        name="localsort",
    )(h2, idx_pad, rank_pad, tile_cnt)


def _ffn_kernel(be_ref, nu_ref, src0_ref, srcn_ref, xl_ref, wg_ref, bg_ref, wu_ref, bu_ref,
                wd_ref, bd_ref, y_ref, xbuf, sem, *, bm, n_blocks):
    del be_ref
    i = pl.program_id(0)
    nu = nu_ref[0]
    slot = i % 2
    n_ch = bm // SUBLANES

    def chunk_copy(src_ref, s, c):
        src_row = pl.multiple_of(src_ref[0, 0, c], SUBLANES)
        return pltpu.make_async_copy(xl_ref.at[pl.ds(src_row, SUBLANES)],
                                     xbuf.at[s, pl.ds(c * SUBLANES, SUBLANES)], sem.at[s])

    def block_wait(s):
        pltpu.make_async_copy(xl_ref.at[pl.ds(0, bm)], xbuf.at[s], sem.at[s]).wait()

    @pl.when(i == 0)
    def _():
        for c in range(n_ch):
            chunk_copy(src0_ref, 0, c).start()

    @pl.when(i < nu)
    def _():
        for c in range(n_ch):
            chunk_copy(srcn_ref, 1 - slot, c).start()
        block_wait(slot)
        x = xbuf[slot].astype(BF16)
        gt = jnp.minimum(jnp.dot(x, wg_ref[0], preferred_element_type=F32) + bg_ref[0],
                         SWIGLU_LIMIT)
        up = jnp.clip(jnp.dot(x, wu_ref[0], preferred_element_type=F32) + bu_ref[0],
                      -SWIGLU_LIMIT, SWIGLU_LIMIT)
        act = gt * jax.nn.sigmoid(SWIGLU_ALPHA * gt) * (up + 1.0)
        y_ref[...] = jnp.dot(act.astype(BF16), wd_ref[0], preferred_element_type=F32) + bd_ref[0]

    @pl.when(i >= nu)
    def _():
        y_ref[...] = jnp.zeros(y_ref.shape, y_ref.dtype)

    @pl.when(i == n_blocks - 1)
    def _():
        block_wait(nu % 2)


def _ffn_call(xl, chunk_src, block_expert, n_used, lw, bm):
    n_blocks = chunk_src.shape[0]
    wmap = lambda i, be, nu: (be[i], 0, 0)
    stab = lambda imap: pl.BlockSpec((1, 1, ROUTER_PAD), imap, memory_space=pltpu.SMEM)
    grid_spec = pltpu.PrefetchScalarGridSpec(
        num_scalar_prefetch=2,
        grid=(n_blocks,),
        in_specs=[stab(lambda i, be, nu: (0, 0, 0)),
                  stab(lambda i, be, nu: (jnp.minimum(i + 1, n_blocks - 1), 0, 0)),
                  pl.BlockSpec(memory_space=pl.ANY),
                  pl.BlockSpec((1, D_MODEL, D_MODEL), wmap), pl.BlockSpec((1, 1, D_MODEL), wmap),
                  pl.BlockSpec((1, D_MODEL, D_MODEL), wmap), pl.BlockSpec((1, 1, D_MODEL), wmap),
                  pl.BlockSpec((1, D_MODEL, D_MODEL), wmap), pl.BlockSpec((1, 1, D_MODEL), wmap)],
        out_specs=pl.BlockSpec((bm, D_MODEL), lambda i, be, nu: (i, 0)),
        scratch_shapes=[pltpu.VMEM((2, bm, D_MODEL), F32), pltpu.SemaphoreType.DMA((2,))],
    )
    return pl.pallas_call(
        functools.partial(_ffn_kernel, bm=bm, n_blocks=n_blocks),
        grid_spec=grid_spec,
        out_shape=jax.ShapeDtypeStruct((n_blocks * bm, D_MODEL), F32),
        compiler_params=_cparams(1),
        name="ffn",
    )(block_expert, n_used, chunk_src, chunk_src, xl, lw["w_gate"], lw["b_gate"], lw["w_up"],
      lw["b_up"], lw["w_down"], lw["b_down"])


def _combine_kernel(gs_ref, cn_ref, gsn_ref, cnn_ref, x_ref, idx_ref, rank_ref, gate_ref, cnt_ref,
                    fn_ref, y_ref, out_ref, ylocal, sem, *, tm, lrows, n_tiles, final):
    i = pl.program_id(0)
    slot = i % 2

    def chunk_copy(s, src_row, dst_row):
        return pltpu.make_async_copy(y_ref.at[pl.ds(src_row, SUBLANES)],
                                     ylocal.at[s, pl.ds(dst_row, SUBLANES)], sem.at[s])

    def request(gs, cn, s):
        def per_expert(e, off):
            n_ch = (cn[0, 0, e] + SUBLANES - 1) // SUBLANES
            src = gs[0, 0, e]

            def per_chunk(c, carry):
                chunk_copy(s, pl.multiple_of(src + c * SUBLANES, SUBLANES),
                           pl.multiple_of(off + c * SUBLANES, SUBLANES)).start()
                return carry

            lax.fori_loop(0, n_ch, per_chunk, 0)
            return off + n_ch * SUBLANES

        lax.fori_loop(0, N_EXPERTS, per_expert, 0)

    @pl.when(i == 0)
    def _():
        ylocal[...] = jnp.zeros(ylocal.shape, ylocal.dtype)
        request(gs_ref, cn_ref, 0)

    @pl.when(i + 1 < n_tiles)
    def _():
        request(gsn_ref, cnn_ref, 1 - slot)

    def count_chunks(e, n):
        return n + (cn_ref[0, 0, e] + SUBLANES - 1) // SUBLANES

    def wait_chunk(c, carry):
        chunk_copy(slot, 0, 0).wait()
        return carry

    lax.fori_loop(0, lax.fori_loop(0, N_EXPERTS, count_chunks, 0), wait_chunk, 0)

    rows = _local_rows(idx_ref[...], rank_ref[...], _run_starts(cnt_ref[0]), tm)
    lane = lax.broadcasted_iota(I32, (tm, ROUTER_PAD), 1)
    gates = gate_ref[...]
    r_iota = lax.broadcasted_iota(I32, (tm, lrows), 1)
    gmat = jnp.zeros((tm, lrows), F32)
    for kk in range(TOP_K):
        g_k = jnp.sum(jnp.where(lane == kk, gates, 0.0), axis=-1, keepdims=True)
        gmat = jnp.where(r_iota == rows[kk], g_k, gmat)
    moe = jnp.dot(gmat.astype(BF16), ylocal[slot].astype(BF16), preferred_element_type=F32)
    acc = x_ref[...] + moe
    if final:
        acc = _rms(acc, fn_ref[...])
    out_ref[...] = acc


def _combine_call(x, idx_pad, rank_pad, gate_pad, tile_cnt, gstart, cnt, y, final_norm, tm,
                  final):
    T = x.shape[0]
    n_tiles = T // tm
    lrows = _local_buffer_rows(tm)
    tab = lambda a: jnp.pad(a, ((0, 0), (0, ROUTER_PAD - N_EXPERTS))).reshape(
        n_tiles, 1, ROUTER_PAD)
    kern = functools.partial(_combine_kernel, tm=tm, lrows=lrows, n_tiles=n_tiles, final=final)
    stab = lambda imap: pl.BlockSpec((1, 1, ROUTER_PAD), imap, memory_space=pltpu.SMEM)
    this_tile = lambda i: (i, 0, 0)
    next_tile = lambda i: (jnp.minimum(i + 1, n_tiles - 1), 0, 0)
    row = lambda w: pl.BlockSpec((tm, w), lambda i: (i, 0))
    return pl.pallas_call(
        kern,
        grid=(n_tiles,),
        in_specs=[stab(this_tile), stab(this_tile), stab(next_tile), stab(next_tile),
                  row(D_MODEL), row(ROUTER_PAD), row(ROUTER_PAD), row(ROUTER_PAD),
                  pl.BlockSpec((1, SUBLANES, ROUTER_PAD), lambda i: (i, 0, 0)),
                  pl.BlockSpec((1, D_MODEL), lambda i: (0, 0)),
                  pl.BlockSpec(memory_space=pl.ANY)],
        out_specs=row(D_MODEL),
        out_shape=jax.ShapeDtypeStruct((T, D_MODEL), F32),
        scratch_shapes=[pltpu.VMEM((2, lrows, D_MODEL), F32), pltpu.SemaphoreType.DMA((2,))],
        compiler_params=_cparams(1),
        name="combine",
    )(tab(gstart), tab(cnt), tab(gstart), tab(cnt), x, idx_pad, rank_pad, gate_pad, tile_cnt,
      final_norm, y)


def _local_buffer_rows(tm):
    return tm * TOP_K + N_EXPERTS * SUBLANES


def _routing_tables(tile_cnt, tm, bm):
    n_tiles = tile_cnt.shape[0]
    lrows = _local_buffer_rows(tm)
    ch_blk = bm // SUBLANES
    cnt = tile_cnt[:, 0, :N_EXPERTS]
    n_ch = (cnt + SUBLANES - 1) // SUBLANES
    reg = jnp.sum(n_ch, axis=0)
    reg_pad = (reg + ch_blk - 1) // ch_blk * ch_blk
    cend = jnp.cumsum(reg_pad)
    cstart = cend - reg_pad
    before_tile = jnp.cumsum(n_ch, axis=0) - n_ch
    gstart = ((cstart[None, :] + before_tile) * SUBLANES).astype(I32)
    lstart = jnp.cumsum(n_ch, axis=1) - n_ch
    max_rows = n_tiles * (tm * TOP_K + N_EXPERTS * (SUBLANES - 1))
    n_blocks = (max_rows + bm - 1) // bm + N_EXPERTS
    blk_first = jnp.arange(n_blocks, dtype=I32) * ch_blk
    block_expert = jnp.minimum(jnp.sum(cend[None, :] <= blk_first[:, None], axis=1),
                               N_EXPERTS - 1).astype(I32)
    n_used = (cend[-1] // ch_blk).astype(I32).reshape(1)
    mine = block_expert[:, None] == jnp.arange(N_EXPERTS, dtype=I32)[None, :]
    pick = lambda tab: jnp.sum(jnp.where(mine[:, None, :], tab[None, :, :], 0), axis=-1)
    cstart_b = jnp.sum(jnp.where(mine, cstart[None, :], 0), axis=-1)
    reg_b = jnp.sum(jnp.where(mine, reg[None, :], 0), axis=-1)
    before_b = pick(before_tile)
    ends_b = pick(before_tile + n_ch)
    lstart_b = pick(lstart)
    o = (blk_first - cstart_b)[:, None] + jnp.arange(ch_blk, dtype=I32)[None, :]
    tile_c = jnp.minimum(jnp.sum(ends_b[:, None, :] <= o[:, :, None], axis=-1), n_tiles - 1)
    at_tile = tile_c[:, :, None] == jnp.arange(n_tiles, dtype=I32)[None, None, :]
    local = jnp.sum(jnp.where(at_tile, (lstart_b - before_b)[:, None, :], 0), axis=-1)
    src = tile_c * (lrows // SUBLANES) + local + o
    src = jnp.where(o < reg_b[:, None], src, lrows // SUBLANES - 1) * SUBLANES
    chunk_src = jnp.pad(src.astype(I32),
                        ((0, 0), (0, ROUTER_PAD - ch_blk))).reshape(n_blocks, 1, ROUTER_PAD)
    return gstart, cnt, chunk_src, block_expert, n_used


def _rope_tables(seq_len):
    inv = 1.0 / (ROPE_THETA ** (jnp.arange(0, HEAD_ROPE, 2, dtype=F32) / HEAD_ROPE))
    ang = jnp.arange(seq_len, dtype=F32)[:, None] * inv[None, :]
    cos, sin = jnp.cos(ang), jnp.sin(ang)
    scale = np.float32(np.log2(np.e) / np.sqrt(HEAD_NOPE + HEAD_ROPE))
    z = jnp.zeros((seq_len, HEAD_PAD - HEAD_NOPE - HEAD_ROPE), F32)
    ones = jnp.ones((seq_len, HEAD_NOPE), F32)
    zeros = jnp.zeros((seq_len, HEAD_NOPE), F32)
    ck = jnp.concatenate([zeros, cos, cos, z], axis=1)
    sk = jnp.concatenate([zeros, sin, sin, z], axis=1)
    cq = jnp.concatenate([ones, cos, cos, z], axis=1) * scale
    sq = sk * scale
    return {"cq": cq, "sq": sq, "ck": ck, "sk": sk}


def _pad_heads(w_nope, w_r1, w_r2):
    lead = w_nope.shape[:-2]
    z = jnp.zeros(lead + (N_HEADS, HEAD_PAD - HEAD_NOPE - HEAD_ROPE), w_nope.dtype)
    return jnp.concatenate([w_nope, w_r1, w_r2, z], axis=-1).reshape(lead + (N_HEADS * HEAD_PAD,))


def _prep_layer(l, ln1, w_in, q_norm, w_uq, kv_norm, w_ukv, w_pool, pool_scale, attn_out_norm,
                pool_out_norm, w_out, ln2, w_router, b_router, w_gate, b_gate, w_up, b_up,
                w_down, b_down):
    half = HEAD_ROPE // 2
    wi = w_in[l]
    kr = wi[:, Q_LORA + KV_LORA:Q_LORA + KV_LORA + HEAD_ROPE]
    kr1, kr2 = kr[:, :half], kr[:, half:]
    zn = jnp.zeros((D_MODEL, HEAD_NOPE), F32)
    zp = jnp.zeros((D_MODEL, HEAD_PAD - HEAD_NOPE - HEAD_ROPE), F32)
    kra = jnp.concatenate([zn, kr1, kr2, zp], axis=1)
    krb = jnp.concatenate([zn, -kr2, kr1, zp], axis=1)
    w_in_ext = jnp.concatenate(
        [wi[:, :Q_LORA + KV_LORA], kra, krb, wi[:, Q_LORA + KV_LORA + HEAD_ROPE:]], axis=1)
    wq = w_uq[l].reshape(Q_LORA, N_HEADS, HEAD_NOPE + HEAD_ROPE)
    qn_, q1, q2 = wq[..., :HEAD_NOPE], wq[..., HEAD_NOPE:HEAD_NOPE + half], wq[..., HEAD_NOPE + half:]
    w_qa = _pad_heads(qn_, q1, q2)
    w_qb = _pad_heads(jnp.zeros_like(qn_), -q2, q1)
    wkv = w_ukv[l].reshape(KV_LORA, N_HEADS, HEAD_NOPE + HEAD_V)
    kz = jnp.zeros((KV_LORA, N_HEADS, half), F32)
    w_k = _pad_heads(wkv[..., :HEAD_NOPE], kz, kz)
    w_v = jnp.pad(wkv[..., HEAD_NOPE:], ((0, 0), (0, 0), (0, HEAD_PAD - HEAD_V))).reshape(
        KV_LORA, N_HEADS * HEAD_PAD)
    wr = jnp.pad(w_router[l], ((0, 0), (0, ROUTER_PAD - N_EXPERTS)))
    wr_hi = wr.astype(BF16)
    wr_lo = (wr - wr_hi.astype(F32)).astype(BF16)
    br = jnp.concatenate([b_router[l].astype(F32),
                          jnp.full((ROUTER_PAD - N_EXPERTS,), -jnp.inf, F32)]).reshape(1, ROUTER_PAD)
    return {
        "ln1": ln1[l].reshape(1, -1), "w_in": w_in_ext.astype(BF16),
        "q_norm": q_norm[l].reshape(1, -1), "w_qa": w_qa.astype(BF16), "w_qb": w_qb.astype(BF16),
        "kv_norm": kv_norm[l].reshape(1, -1), "w_k": w_k.astype(BF16), "w_v": w_v.astype(BF16),
        "w_pool": w_pool[l].astype(BF16), "pool_scale": pool_scale[l].reshape(1, -1),
        "attn_out_norm": attn_out_norm[l].reshape(1, -1),
        "pool_out_norm": pool_out_norm[l].reshape(1, -1),
        "w_out": w_out[l].astype(BF16), "ln2": ln2[l].reshape(1, -1),
        "w_r_hi": wr_hi, "w_r_lo": wr_lo, "b_r": br,
        "w_gate": w_gate[l].astype(BF16), "b_gate": b_gate[l].reshape(N_EXPERTS, 1, -1),
        "w_up": w_up[l].astype(BF16), "b_up": b_up[l].reshape(N_EXPERTS, 1, -1),
        "w_down": w_down[l].astype(BF16), "b_down": b_down[l].reshape(N_EXPERTS, 1, -1),
    }


def _tiles(seq_len):
    tm = min(512, seq_len)
    tq = min(512, seq_len)
    n_sub = min(2048, seq_len) // tq
    tk = min(2048, seq_len)
    bm = 512
    return tm, tq, n_sub, tk, bm


def _trunk(x3, layers, final_norm):
    B, S, _ = x3.shape
    T = B * S
    tm, tq, n_sub, tk, bm = _tiles(S)
    tabs = _rope_tables(S)
    x = x3.reshape(T, D_MODEL)
    fn = final_norm.reshape(1, -1)
    for li, lw in enumerate(layers):
        q, k, v, u = _proj_call(x, lw, tabs, S, tm)
        o = _attn_call(q, k, v, B, S, tq, tk, n_sub)
        x, h2, idx_pad, gate_pad, rank_pad, tile_cnt = _mix_call(x, o, u, lw, S, tm)
        gstart, cnt, chunk_src, block_expert, n_used = _routing_tables(tile_cnt, tm, bm)
        xl = _localsort_call(h2, idx_pad, rank_pad, tile_cnt, tm)
        y = _ffn_call(xl, chunk_src, block_expert, n_used, lw, bm)
        x = _combine_call(x, idx_pad, rank_pad, gate_pad, tile_cnt, gstart, cnt, y, fn, tm,
                          final=(li == len(layers) - 1))
    return x.reshape(B, S, D_MODEL)


def kernel(x_prompt, x_sample, ln1, w_in, q_norm, w_uq, kv_norm, w_ukv, w_pool, pool_scale,
           attn_out_norm, pool_out_norm, w_out, ln2, w_router, b_router, w_gate, b_gate, w_up,
           b_up, w_down, b_down, final_norm):
    params = (ln1, w_in, q_norm, w_uq, kv_norm, w_ukv, w_pool, pool_scale, attn_out_norm,
              pool_out_norm, w_out, ln2, w_router, b_router, w_gate, b_gate, w_up, b_up,
              w_down, b_down)
    layers = [_prep_layer(l, *params) for l in range(ln1.shape[0])]
    return (_trunk(x_prompt, layers, final_norm), _trunk(x_sample, layers, final_norm))
```

```python
import functools

import jax
import jax.numpy as jnp
import numpy as np
from jax import lax
from jax.experimental import pallas as pl
from jax.experimental.pallas import tpu as pltpu

F32 = jnp.float32
BF16 = jnp.bfloat16
I32 = jnp.int32

D_MODEL = 1024
N_HEADS = 8
HEAD_NOPE = 64
HEAD_ROPE = 32
HEAD_V = 64
HEAD_PAD = 128
Q_LORA = 256
KV_LORA = 128
ATT_W = 512
POOL_W = 512
POOL_WINDOWS = (2, 4, 8, 16)
POOL_GW = 128
POOL_HALO = 8
ROPE_THETA = 10000.0
N_EXPERTS = 32
TOP_K = 4
SWIGLU_LIMIT = 7.0
SWIGLU_ALPHA = 1.702
NORM_EPS = 1e-5
ROUTER_PAD = 128
SUBLANES = 8

_C_Q = 0
_C_KV = _C_Q + Q_LORA
_C_KRA = _C_KV + KV_LORA
_C_KRB = _C_KRA + HEAD_PAD
_C_U = _C_KRB + HEAD_PAD
IN_W_EXT = _C_U + POOL_W

VMEM_LIMIT = 56 * 1024 * 1024


def _cparams(n_axes):
    return pltpu.CompilerParams(dimension_semantics=("arbitrary",) * n_axes,
                                vmem_limit_bytes=VMEM_LIMIT)


def _rms(x, g):
    return x * lax.rsqrt(jnp.mean(x * x, axis=-1, keepdims=True) + NORM_EPS) * g


def _proj_kernel(x_ref, ln1_ref, win_ref, qn_ref, wqa_ref, wqb_ref, kvn_ref, wk_ref, wv_ref,
                 cq_ref, sq_ref, ck_ref, sk_ref, q_ref, k_ref, v_ref, u_ref):
    h = _rms(x_ref[...], ln1_ref[...]).astype(BF16)
    proj = jnp.dot(h, win_ref[...], preferred_element_type=F32)
    u_ref[...] = proj[:, _C_U:_C_U + POOL_W]
    hq = _rms(proj[:, _C_Q:_C_Q + Q_LORA], qn_ref[...]).astype(BF16)
    qa = jnp.dot(hq, wqa_ref[...], preferred_element_type=F32)
    qb = jnp.dot(hq, wqb_ref[...], preferred_element_type=F32)
    hkv = _rms(proj[:, _C_KV:_C_KV + KV_LORA], kvn_ref[...]).astype(BF16)
    kn = jnp.dot(hkv, wk_ref[...], preferred_element_type=F32)
    vv = jnp.dot(hkv, wv_ref[...], preferred_element_type=F32)
    lane = lax.broadcasted_iota(I32, vv.shape, 1)
    v_ref[...] = jnp.where((lane & (HEAD_PAD - 1)) == HEAD_V, 1.0, vv).astype(BF16)
    k_rope = (proj[:, _C_KRA:_C_KRA + HEAD_PAD] * ck_ref[...]
              + proj[:, _C_KRB:_C_KRB + HEAD_PAD] * sk_ref[...])
    cq = cq_ref[...]
    sq = sq_ref[...]
    for hd in range(N_HEADS):
        sl = slice(hd * HEAD_PAD, (hd + 1) * HEAD_PAD)
        q_ref[:, sl] = (qa[:, sl] * cq + qb[:, sl] * sq).astype(BF16)
        k_ref[:, sl] = (kn[:, sl] + k_rope).astype(BF16)


def _proj_call(x, lw, tabs, seq_len, tm):
    T = x.shape[0]
    tiles_per_seq = seq_len // tm
    full = lambda shape: pl.BlockSpec(shape, lambda i: (0,) * len(shape))
    tab = pl.BlockSpec((tm, HEAD_PAD), lambda i: (i % tiles_per_seq, 0))
    row = lambda w: pl.BlockSpec((tm, w), lambda i: (i, 0))
    return pl.pallas_call(
        _proj_kernel,
        grid=(T // tm,),
        in_specs=[row(D_MODEL), full((1, D_MODEL)), full((D_MODEL, IN_W_EXT)),
                  full((1, Q_LORA)), full((Q_LORA, N_HEADS * HEAD_PAD)),
                  full((Q_LORA, N_HEADS * HEAD_PAD)),
                  full((1, KV_LORA)), full((KV_LORA, N_HEADS * HEAD_PAD)),
                  full((KV_LORA, N_HEADS * HEAD_PAD)), tab, tab, tab, tab],
        out_specs=[row(N_HEADS * HEAD_PAD), row(N_HEADS * HEAD_PAD), row(N_HEADS * HEAD_PAD),
                   row(POOL_W)],
        out_shape=[jax.ShapeDtypeStruct((T, N_HEADS * HEAD_PAD), BF16),
                   jax.ShapeDtypeStruct((T, N_HEADS * HEAD_PAD), BF16),
                   jax.ShapeDtypeStruct((T, N_HEADS * HEAD_PAD), BF16),
                   jax.ShapeDtypeStruct((T, POOL_W), F32)],
        compiler_params=_cparams(1),
        name="proj",
    )(x, lw["ln1"], lw["w_in"], lw["q_norm"], lw["w_qa"], lw["w_qb"], lw["kv_norm"],
      lw["w_k"], lw["w_v"], tabs["cq"], tabs["sq"], tabs["ck"], tabs["sk"])


def _attn_kernel(q_ref, k_ref, v_ref, o_ref, m_scr, acc_scr, sa_scr, *, tq, tk, n_sub, n_chunks):
    m_scr[...] = jnp.full(m_scr.shape, -jnp.inf, F32)
    acc_scr[...] = jnp.zeros(acc_scr.shape, F32)
    n_steps = n_sub * n_chunks

    def head_cols(hh):
        return slice(hh * HEAD_PAD, (hh + 1) * HEAD_PAD)

    def block(i, size):
        return pl.ds(i * size if isinstance(i, int) else pl.multiple_of(i * size, size), size)

    def split(n):
        return (n // n_chunks, n % n_chunks) if isinstance(n, int) else (
            lax.div(n, n_chunks), lax.rem(n, n_chunks))

    def scores(hh, n):
        sub, j = split(n)
        kc = k_ref[block(j, tk), head_cols(hh)]
        return lax.dot_general(q_ref[block(sub, tq), head_cols(hh)], kc,
                               (((1,), (1,)), ((), ())),
                               preferred_element_type=F32)

    def update(hh, s, n):
        sub, j = split(n)
        vc = v_ref[block(j, tk), head_cols(hh)]
        m_old = m_scr[sub, hh]
        m_new = jnp.maximum(m_old, jnp.max(s, axis=-1, keepdims=True))
        alpha = jnp.exp2(m_old - m_new)
        p = jnp.exp2(s - m_new).astype(BF16)
        acc_scr[sub, hh] = alpha * acc_scr[sub, hh] + jnp.dot(p, vc, preferred_element_type=F32)
        m_scr[sub, hh] = m_new

    sa_scr[...] = scores(0, 0)

    def step(n, carry):
        sb = scores(1, n)
        update(0, sa_scr[...], n)
        sa_scr[...] = scores(0, n + 1)
        update(1, sb, n)
        return carry

    lax.fori_loop(0, n_steps - 1, step, 0, unroll=2)
    sb = scores(1, n_steps - 1)
    update(0, sa_scr[...], n_steps - 1)
    update(1, sb, n_steps - 1)
    for sub in range(n_sub):
        outs = []
        for hh in range(2):
            acc = acc_scr[sub, hh]
            outs.append(acc[:, :HEAD_V] / acc[:, HEAD_V:HEAD_V + 1])
        o_ref[sub * tq:(sub + 1) * tq, :] = jnp.concatenate(outs, axis=-1).astype(o_ref.dtype)


def _attn_call(q, k, v, batch, seq_len, tq, tk, n_sub):
    T = q.shape[0]
    rows = tq * n_sub
    n_q = seq_len // rows
    kern = functools.partial(_attn_kernel, tq=tq, tk=tk, n_sub=n_sub, n_chunks=seq_len // tk)
    pair = lambda r: pl.BlockSpec((r, 2 * HEAD_PAD), lambda b, hp, i: (b, hp),
                                  pipeline_mode=pl.Buffered(1))
    return pl.pallas_call(
        kern,
        grid=(batch, N_HEADS // 2, n_q),
        in_specs=[pl.BlockSpec((rows, 2 * HEAD_PAD), lambda b, hp, i: (b * n_q + i, hp)),
                  pair(seq_len), pair(seq_len)],
        out_specs=pl.BlockSpec((rows, 2 * HEAD_V), lambda b, hp, i: (b * n_q + i, hp)),
        out_shape=jax.ShapeDtypeStruct((T, ATT_W), BF16),
        scratch_shapes=[pltpu.VMEM((n_sub, 2, tq, 1), F32),
                        pltpu.VMEM((n_sub, 2, tq, HEAD_PAD), F32),
                        pltpu.VMEM((tq, tk), F32)],
        compiler_params=_cparams(3),
        name="attn",
    )(q, k, v)


def _mix_kernel(x_ref, o_ref, u_ref, up_ref, un_ref, an_ref, pn_ref, wpool_ref, pscale_ref,
                wout_ref, ln2_ref, wrh_ref, wrl_ref, br_ref,
                xn_ref, h2_ref, idx_ref, gate_ref, rank_ref, cnt_ref, ubuf, *, tm, seq_len):
    i = pl.program_id(0)
    tiles_per_seq = seq_len // tm
    ti = i % tiles_per_seq
    prev_ok = jnp.where(ti > 0, 1.0, 0.0).astype(F32)
    next_ok = jnp.where(ti < tiles_per_seq - 1, 1.0, 0.0).astype(F32)
    ubuf[0:POOL_HALO, :] = up_ref[...] * prev_ok
    ubuf[POOL_HALO:POOL_HALO + tm, :] = u_ref[...]
    ubuf[POOL_HALO + tm:2 * POOL_HALO + tm, :] = un_ref[...] * next_ok

    pos = ti * tm + lax.broadcasted_iota(I32, (tm, 1), 0)
    pooled = []
    for g, w in enumerate(POOL_WINDOWS):
        cols = slice(g * POOL_GW, (g + 1) * POOL_GW)
        tot = jnp.zeros((tm, POOL_GW), F32)
        for d in range(-(w // 2), w - w // 2):
            tot = tot + ubuf[POOL_HALO + d:POOL_HALO + d + tm, cols]
        lo = jnp.maximum(pos - w // 2, 0)
        hi = jnp.minimum(pos + (w - w // 2), seq_len)
        cnt = (hi - lo).astype(F32)
        pg = (tot / cnt - u_ref[:, cols]).astype(BF16)
        yg = jnp.dot(pg, wpool_ref[g], preferred_element_type=F32) * pscale_ref[:, cols]
        pooled.append(yg)
    pool = jnp.concatenate(pooled, axis=-1)
    attn = o_ref[...].astype(F32)
    mixed = jnp.concatenate([_rms(attn, an_ref[...]), _rms(pool, pn_ref[...])],
                            axis=-1).astype(BF16)
    xn = x_ref[...] + jnp.dot(mixed, wout_ref[...], preferred_element_type=F32)
    xn_ref[...] = xn
    h2 = _rms(xn, ln2_ref[...])
    hi_ = h2.astype(BF16)
    h2_ref[...] = hi_
    lo_ = (h2 - hi_.astype(F32)).astype(BF16)
    logits = (jnp.dot(hi_, wrh_ref[...], preferred_element_type=F32)
              + jnp.dot(lo_, wrh_ref[...], preferred_element_type=F32)
              + jnp.dot(hi_, wrl_ref[...], preferred_element_type=F32)) + br_ref[...]
    lane = lax.broadcasted_iota(I32, logits.shape, 1)
    vals, idxs = [], []
    for _ in range(TOP_K):
        mx = jnp.max(logits, axis=-1, keepdims=True)
        ix = jnp.min(jnp.where(logits == mx, lane, ROUTER_PAD), axis=-1, keepdims=True)
        vals.append(mx)
        idxs.append(ix)
        logits = jnp.where(lane == ix, -jnp.inf, logits)
    es = [jnp.exp(vk - vals[0]) for vk in vals]
    den = es[0] + es[1] + es[2] + es[3]
    hits = [lane == idxs[kk] for kk in range(TOP_K)]
    routed = jnp.where(hits[0] | hits[1] | hits[2] | hits[3], 1.0, 0.0)
    r_i = lax.broadcasted_iota(I32, (tm, tm), 0)
    c_i = lax.broadcasted_iota(I32, (tm, tm), 1)
    ltri = jnp.where(c_i < r_i, 1.0, 0.0).astype(BF16)
    before = jnp.dot(ltri, routed.astype(BF16), preferred_element_type=F32)
    idx_out = jnp.zeros(lane.shape, I32)
    gate_out = jnp.zeros(lane.shape, F32)
    rank_out = jnp.zeros(lane.shape, I32)
    for kk in range(TOP_K):
        rk = jnp.sum(jnp.where(hits[kk], before, 0.0), axis=-1, keepdims=True).astype(I32)
        idx_out = jnp.where(lane == kk, idxs[kk], idx_out)
        gate_out = jnp.where(lane == kk, es[kk] / den, gate_out)
        rank_out = jnp.where(lane == kk, rk, rank_out)
    idx_ref[...] = idx_out
    gate_ref[...] = gate_out
    rank_ref[...] = rank_out
    counts = jnp.sum(routed, axis=0, keepdims=True).astype(I32)
    cnt_ref[0] = jnp.broadcast_to(counts, (SUBLANES, ROUTER_PAD))


def _mix_call(x, o, u, lw, seq_len, tm):
    T = x.shape[0]
    hb = tm // POOL_HALO
    n_hblk = T // POOL_HALO
    full = lambda shape: pl.BlockSpec(shape, lambda i: (0,) * len(shape))
    row = lambda w: pl.BlockSpec((tm, w), lambda i: (i, 0))
    kern = functools.partial(_mix_kernel, tm=tm, seq_len=seq_len)
    return pl.pallas_call(
        kern,
        grid=(T // tm,),
        in_specs=[row(D_MODEL), row(ATT_W), row(POOL_W),
                  pl.BlockSpec((POOL_HALO, POOL_W), lambda i: (jnp.maximum(i * hb - 1, 0), 0)),
                  pl.BlockSpec((POOL_HALO, POOL_W),
                               lambda i: (jnp.minimum((i + 1) * hb, n_hblk - 1), 0)),
                  full((1, ATT_W)), full((1, POOL_W)), full((4, POOL_GW, POOL_GW)),
                  full((1, POOL_W)), full((ATT_W + POOL_W, D_MODEL)), full((1, D_MODEL)),
                  full((D_MODEL, ROUTER_PAD)), full((D_MODEL, ROUTER_PAD)),
                  full((1, ROUTER_PAD))],
        out_specs=[row(D_MODEL), row(D_MODEL), row(ROUTER_PAD), row(ROUTER_PAD), row(ROUTER_PAD),
                   pl.BlockSpec((1, SUBLANES, ROUTER_PAD), lambda i: (i, 0, 0))],
        out_shape=[jax.ShapeDtypeStruct((T, D_MODEL), F32),
                   jax.ShapeDtypeStruct((T, D_MODEL), BF16),
                   jax.ShapeDtypeStruct((T, ROUTER_PAD), I32),
                   jax.ShapeDtypeStruct((T, ROUTER_PAD), F32),
                   jax.ShapeDtypeStruct((T, ROUTER_PAD), I32),
                   jax.ShapeDtypeStruct((T // tm, SUBLANES, ROUTER_PAD), I32)],
        scratch_shapes=[pltpu.VMEM((tm + 2 * POOL_HALO, POOL_W), F32)],
        compiler_params=_cparams(1),
        name="mix",
    )(x, o, u, u, u, lw["attn_out_norm"], lw["pool_out_norm"], lw["w_pool"], lw["pool_scale"],
      lw["w_out"], lw["ln2"], lw["w_r_hi"], lw["w_r_lo"], lw["b_r"])


def _run_starts(cnt_tile):
    n_ch_e = ((cnt_tile + SUBLANES - 1) // SUBLANES).astype(F32)
    e_r = lax.broadcasted_iota(I32, (ROUTER_PAD, ROUTER_PAD), 0)
    e_c = lax.broadcasted_iota(I32, (ROUTER_PAD, ROUTER_PAD), 1)
    before = jnp.where(e_r < e_c, 1.0, 0.0).astype(BF16)
    starts = jnp.dot(n_ch_e.astype(BF16), before, preferred_element_type=F32)[0:1, :]
    return starts * float(SUBLANES)


def _local_rows(idx, rank, run_start, tm):
    lane = lax.broadcasted_iota(I32, (tm, ROUTER_PAD), 1)
    rows = []
    for kk in range(TOP_K):
        e_k = jnp.sum(jnp.where(lane == kk, idx, 0), axis=-1, keepdims=True)
        start_k = jnp.sum(jnp.where(lane == e_k, run_start, 0.0), axis=-1, keepdims=True)
        rank_k = jnp.sum(jnp.where(lane == kk, rank, 0), axis=-1, keepdims=True)
        rows.append(start_k.astype(I32) + rank_k)
    return rows


def _localsort_kernel(h_ref, idx_ref, rank_ref, cnt_ref, xl_ref, *, tm, lrows):
    rows = _local_rows(idx_ref[...], rank_ref[...], _run_starts(cnt_ref[0]), tm)
    lane = lax.broadcasted_iota(I32, (tm, ROUTER_PAD), 1)
    rows_pad = jnp.full((tm, ROUTER_PAD), -1.0, F32)
    for kk in range(TOP_K):
        rows_pad = jnp.where(lane == kk, rows[kk].astype(F32), rows_pad)
    rows_t = rows_pad.T
    r_iota = lax.broadcasted_iota(I32, (lrows, tm), 0).astype(F32)
    sel = jnp.zeros((lrows, tm), F32)
    for kk in range(TOP_K):
        sel = jnp.where(r_iota == rows_t[kk:kk + 1, :], 1.0, sel)
    xl_ref[...] = jnp.dot(sel.astype(BF16), h_ref[...], preferred_element_type=F32)


def _localsort_call(h2, idx_pad, rank_pad, tile_cnt, tm):
    T = h2.shape[0]
    n_tiles = T // tm
    lrows = _local_buffer_rows(tm)
    row = lambda w: pl.BlockSpec((tm, w), lambda i: (i, 0))
    return pl.pallas_call(
        functools.partial(_localsort_kernel, tm=tm, lrows=lrows),
        grid=(n_tiles,),
        in_specs=[row(D_MODEL), row(ROUTER_PAD), row(ROUTER_PAD),
                  pl.BlockSpec((1, SUBLANES, ROUTER_PAD), lambda i: (i, 0, 0))],
        out_specs=pl.BlockSpec((lrows, D_MODEL), lambda i: (i, 0)),
        out_shape=jax.ShapeDtypeStruct((n_tiles * lrows, D_MODEL), F32),
        compiler_params=_cparams(1),
        name="localsort",
    )(h2, idx_pad, rank_pad, tile_cnt)


def _ffn_kernel(be_ref, nu_ref, src0_ref, srcn_ref, xl_ref, wg_ref, bg_ref, wu_ref, bu_ref,
                wd_ref, bd_ref, y_ref, xbuf, sem, *, bm, n_blocks):
    del be_ref
    i = pl.program_id(0)
    nu = nu_ref[0]
    slot = i % 2
    n_ch = bm // SUBLANES

    def chunk_copy(src_ref, s, c):
        src_row = pl.multiple_of(src_ref[0, 0, c], SUBLANES)
        return pltpu.make_async_copy(xl_ref.at[pl.ds(src_row, SUBLANES)],
                                     xbuf.at[s, pl.ds(c * SUBLANES, SUBLANES)], sem.at[s])

    def block_wait(s):
        pltpu.make_async_copy(xl_ref.at[pl.ds(0, bm)], xbuf.at[s], sem.at[s]).wait()

    @pl.when(i == 0)
    def _():
        for c in range(n_ch):
            chunk_copy(src0_ref, 0, c).start()

    @pl.when(i < nu)
    def _():
        for c in range(n_ch):
            chunk_copy(srcn_ref, 1 - slot, c).start()
        block_wait(slot)
        x = xbuf[slot].astype(BF16)
        gt = jnp.minimum(jnp.dot(x, wg_ref[0], preferred_element_type=F32) + bg_ref[0],
                         SWIGLU_LIMIT)
        up = jnp.clip(jnp.dot(x, wu_ref[0], preferred_element_type=F32) + bu_ref[0],
                      -SWIGLU_LIMIT, SWIGLU_LIMIT)
        act = gt * jax.nn.sigmoid(SWIGLU_ALPHA * gt) * (up + 1.0)
        y_ref[...] = jnp.dot(act.astype(BF16), wd_ref[0], preferred_element_type=F32) + bd_ref[0]

    @pl.when(i >= nu)
    def _():
        y_ref[...] = jnp.zeros(y_ref.shape, y_ref.dtype)

    @pl.when(i == n_blocks - 1)
    def _():
        block_wait(nu % 2)


def _ffn_call(xl, chunk_src, block_expert, n_used, lw, bm):
    n_blocks = chunk_src.shape[0]
    wmap = lambda i, be, nu: (be[i], 0, 0)
    stab = lambda imap: pl.BlockSpec((1, 1, ROUTER_PAD), imap, memory_space=pltpu.SMEM)
    grid_spec = pltpu.PrefetchScalarGridSpec(
        num_scalar_prefetch=2,
        grid=(n_blocks,),
        in_specs=[stab(lambda i, be, nu: (0, 0, 0)),
                  stab(lambda i, be, nu: (jnp.minimum(i + 1, n_blocks - 1), 0, 0)),
                  pl.BlockSpec(memory_space=pl.ANY),
                  pl.BlockSpec((1, D_MODEL, D_MODEL), wmap), pl.BlockSpec((1, 1, D_MODEL), wmap),
                  pl.BlockSpec((1, D_MODEL, D_MODEL), wmap), pl.BlockSpec((1, 1, D_MODEL), wmap),
                  pl.BlockSpec((1, D_MODEL, D_MODEL), wmap), pl.BlockSpec((1, 1, D_MODEL), wmap)],
        out_specs=pl.BlockSpec((bm, D_MODEL), lambda i, be, nu: (i, 0)),
        scratch_shapes=[pltpu.VMEM((2, bm, D_MODEL), F32), pltpu.SemaphoreType.DMA((2,))],
    )
    return pl.pallas_call(
        functools.partial(_ffn_kernel, bm=bm, n_blocks=n_blocks),
        grid_spec=grid_spec,
        out_shape=jax.ShapeDtypeStruct((n_blocks * bm, D_MODEL), F32),
        compiler_params=_cparams(1),
        name="ffn",
    )(block_expert, n_used, chunk_src, chunk_src, xl, lw["w_gate"], lw["b_gate"], lw["w_up"],
      lw["b_up"], lw["w_down"], lw["b_down"])


def _combine_kernel(src_ref, srcn_ref, x_ref, idx_ref, rank_ref, gate_ref, cnt_ref, fn_ref, y_ref,
                    out_ref, ylocal, sem, *, tm, lrows, n_tiles, final):
    i = pl.program_id(0)
    slot = i % 2
    n_ch = lrows // SUBLANES

    def request(table, s):
        def per_chunk(c, carry):
            src_row = pl.multiple_of(table[0, 0, c], SUBLANES)
            dst_row = pl.multiple_of(c * SUBLANES, SUBLANES)
            pltpu.make_async_copy(y_ref.at[pl.ds(src_row, SUBLANES)],
                                  ylocal.at[s, pl.ds(dst_row, SUBLANES)], sem.at[s]).start()
            return carry

        lax.fori_loop(0, n_ch, per_chunk, 0, unroll=8)

    @pl.when(i == 0)
    def _():
        request(src_ref, 0)

    @pl.when(i + 1 < n_tiles)
    def _():
        request(srcn_ref, 1 - slot)

    pltpu.make_async_copy(y_ref.at[pl.ds(0, lrows)], ylocal.at[slot], sem.at[slot]).wait()

    rows = _local_rows(idx_ref[...], rank_ref[...], _run_starts(cnt_ref[0]), tm)
    lane = lax.broadcasted_iota(I32, (tm, ROUTER_PAD), 1)
    gates = gate_ref[...]
    r_iota = lax.broadcasted_iota(I32, (tm, lrows), 1)
    gmat = jnp.zeros((tm, lrows), F32)
    for kk in range(TOP_K):
        g_k = jnp.sum(jnp.where(lane == kk, gates, 0.0), axis=-1, keepdims=True)
        gmat = jnp.where(r_iota == rows[kk], g_k, gmat)
    moe = jnp.dot(gmat.astype(BF16), ylocal[slot].astype(BF16), preferred_element_type=F32)
    acc = x_ref[...] + moe
    if final:
        acc = _rms(acc, fn_ref[...])
    out_ref[...] = acc


def _combine_call(x, idx_pad, rank_pad, gate_pad, tile_cnt, comb_src, y, final_norm, tm, final):
    T = x.shape[0]
    n_tiles = T // tm
    lrows = _local_buffer_rows(tm)
    kern = functools.partial(_combine_kernel, tm=tm, lrows=lrows, n_tiles=n_tiles, final=final)
    stab = lambda imap: pl.BlockSpec((1, 1, comb_src.shape[-1]), imap, memory_space=pltpu.SMEM)
    row = lambda w: pl.BlockSpec((tm, w), lambda i: (i, 0))
    return pl.pallas_call(
        kern,
        grid=(n_tiles,),
        in_specs=[stab(lambda i: (i, 0, 0)),
                  stab(lambda i: (jnp.minimum(i + 1, n_tiles - 1), 0, 0)),
                  row(D_MODEL), row(ROUTER_PAD), row(ROUTER_PAD), row(ROUTER_PAD),
                  pl.BlockSpec((1, SUBLANES, ROUTER_PAD), lambda i: (i, 0, 0)),
                  pl.BlockSpec((1, D_MODEL), lambda i: (0, 0)),
                  pl.BlockSpec(memory_space=pl.ANY)],
        out_specs=row(D_MODEL),
        out_shape=jax.ShapeDtypeStruct((T, D_MODEL), F32),
        scratch_shapes=[pltpu.VMEM((2, lrows, D_MODEL), F32), pltpu.SemaphoreType.DMA((2,))],
        compiler_params=_cparams(1),
        name="combine",
    )(comb_src, comb_src, x, idx_pad, rank_pad, gate_pad, tile_cnt, final_norm, y)


def _local_buffer_rows(tm):
    return tm * TOP_K + N_EXPERTS * SUBLANES


def _routing_tables(tile_cnt, tm, bm):
    n_tiles = tile_cnt.shape[0]
    lrows = _local_buffer_rows(tm)
    ch_blk = bm // SUBLANES
    cnt = tile_cnt[:, 0, :N_EXPERTS]
    n_ch = (cnt + SUBLANES - 1) // SUBLANES
    reg = jnp.sum(n_ch, axis=0)
    reg_pad = (reg + ch_blk - 1) // ch_blk * ch_blk
    cend = jnp.cumsum(reg_pad)
    cstart = cend - reg_pad
    before_tile = jnp.cumsum(n_ch, axis=0) - n_ch
    gstart = ((cstart[None, :] + before_tile) * SUBLANES).astype(I32)
    lstart = jnp.cumsum(n_ch, axis=1) - n_ch
    max_rows = n_tiles * (tm * TOP_K + N_EXPERTS * (SUBLANES - 1))
    n_blocks = (max_rows + bm - 1) // bm + N_EXPERTS
    blk_first = jnp.arange(n_blocks, dtype=I32) * ch_blk
    block_expert = jnp.minimum(jnp.sum(cend[None, :] <= blk_first[:, None], axis=1),
                               N_EXPERTS - 1).astype(I32)
    n_used = (cend[-1] // ch_blk).astype(I32).reshape(1)
    mine = block_expert[:, None] == jnp.arange(N_EXPERTS, dtype=I32)[None, :]
    pick = lambda tab: jnp.sum(jnp.where(mine[:, None, :], tab[None, :, :], 0), axis=-1)
    cstart_b = jnp.sum(jnp.where(mine, cstart[None, :], 0), axis=-1)
    reg_b = jnp.sum(jnp.where(mine, reg[None, :], 0), axis=-1)
    before_b = pick(before_tile)
    ends_b = pick(before_tile + n_ch)
    lstart_b = pick(lstart)
    o = (blk_first - cstart_b)[:, None] + jnp.arange(ch_blk, dtype=I32)[None, :]
    tile_c = jnp.minimum(jnp.sum(ends_b[:, None, :] <= o[:, :, None], axis=-1), n_tiles - 1)
    at_tile = tile_c[:, :, None] == jnp.arange(n_tiles, dtype=I32)[None, None, :]
    local = jnp.sum(jnp.where(at_tile, (lstart_b - before_b)[:, None, :], 0), axis=-1)
    src = tile_c * (lrows // SUBLANES) + local + o
    src = jnp.where(o < reg_b[:, None], src, lrows // SUBLANES - 1) * SUBLANES
    chunk_src = jnp.pad(src.astype(I32),
                        ((0, 0), (0, ROUTER_PAD - ch_blk))).reshape(n_blocks, 1, ROUTER_PAD)
    lc = jnp.arange(lrows // SUBLANES, dtype=I32)[None, :, None]
    in_run = (lstart[:, None, :] <= lc) & (lc < (lstart + n_ch)[:, None, :])
    back = jnp.sum(jnp.where(in_run, gstart[:, None, :] + (lc - lstart[:, None, :]) * SUBLANES,
                             0), axis=-1).astype(I32)
    lpad = -(lrows // SUBLANES) % ROUTER_PAD
    comb_src = jnp.pad(back, ((0, 0), (0, lpad))).reshape(n_tiles, 1, -1)
    return chunk_src, comb_src, block_expert, n_used


def _rope_tables(seq_len):
    inv = 1.0 / (ROPE_THETA ** (jnp.arange(0, HEAD_ROPE, 2, dtype=F32) / HEAD_ROPE))
    ang = jnp.arange(seq_len, dtype=F32)[:, None] * inv[None, :]
    cos, sin = jnp.cos(ang), jnp.sin(ang)
    scale = np.float32(np.log2(np.e) / np.sqrt(HEAD_NOPE + HEAD_ROPE))
    z = jnp.zeros((seq_len, HEAD_PAD - HEAD_NOPE - HEAD_ROPE), F32)
    ones = jnp.ones((seq_len, HEAD_NOPE), F32)
    zeros = jnp.zeros((seq_len, HEAD_NOPE), F32)
    ck = jnp.concatenate([zeros, cos, cos, z], axis=1)
    sk = jnp.concatenate([zeros, sin, sin, z], axis=1)
    cq = jnp.concatenate([ones, cos, cos, z], axis=1) * scale
    sq = sk * scale
    return {"cq": cq, "sq": sq, "ck": ck, "sk": sk}


def _pad_heads(w_nope, w_r1, w_r2):
    lead = w_nope.shape[:-2]
    z = jnp.zeros(lead + (N_HEADS, HEAD_PAD - HEAD_NOPE - HEAD_ROPE), w_nope.dtype)
    return jnp.concatenate([w_nope, w_r1, w_r2, z], axis=-1).reshape(lead + (N_HEADS * HEAD_PAD,))


def _prep_layer(l, ln1, w_in, q_norm, w_uq, kv_norm, w_ukv, w_pool, pool_scale, attn_out_norm,
                pool_out_norm, w_out, ln2, w_router, b_router, w_gate, b_gate, w_up, b_up,
                w_down, b_down):
    half = HEAD_ROPE // 2
    wi = w_in[l]
    kr = wi[:, Q_LORA + KV_LORA:Q_LORA + KV_LORA + HEAD_ROPE]
    kr1, kr2 = kr[:, :half], kr[:, half:]
    zn = jnp.zeros((D_MODEL, HEAD_NOPE), F32)
    zp = jnp.zeros((D_MODEL, HEAD_PAD - HEAD_NOPE - HEAD_ROPE), F32)
    kra = jnp.concatenate([zn, kr1, kr2, zp], axis=1)
    krb = jnp.concatenate([zn, -kr2, kr1, zp], axis=1)
    w_in_ext = jnp.concatenate(
        [wi[:, :Q_LORA + KV_LORA], kra, krb, wi[:, Q_LORA + KV_LORA + HEAD_ROPE:]], axis=1)
    wq = w_uq[l].reshape(Q_LORA, N_HEADS, HEAD_NOPE + HEAD_ROPE)
    qn_, q1, q2 = wq[..., :HEAD_NOPE], wq[..., HEAD_NOPE:HEAD_NOPE + half], wq[..., HEAD_NOPE + half:]
    w_qa = _pad_heads(qn_, q1, q2)
    w_qb = _pad_heads(jnp.zeros_like(qn_), -q2, q1)
    wkv = w_ukv[l].reshape(KV_LORA, N_HEADS, HEAD_NOPE + HEAD_V)
    kz = jnp.zeros((KV_LORA, N_HEADS, half), F32)
    w_k = _pad_heads(wkv[..., :HEAD_NOPE], kz, kz)
    w_v = jnp.pad(wkv[..., HEAD_NOPE:], ((0, 0), (0, 0), (0, HEAD_PAD - HEAD_V))).reshape(
        KV_LORA, N_HEADS * HEAD_PAD)
    wr = jnp.pad(w_router[l], ((0, 0), (0, ROUTER_PAD - N_EXPERTS)))
    wr_hi = wr.astype(BF16)
    wr_lo = (wr - wr_hi.astype(F32)).astype(BF16)
    br = jnp.concatenate([b_router[l].astype(F32),
                          jnp.full((ROUTER_PAD - N_EXPERTS,), -jnp.inf, F32)]).reshape(1, ROUTER_PAD)
    return {
        "ln1": ln1[l].reshape(1, -1), "w_in": w_in_ext.astype(BF16),
        "q_norm": q_norm[l].reshape(1, -1), "w_qa": w_qa.astype(BF16), "w_qb": w_qb.astype(BF16),
        "kv_norm": kv_norm[l].reshape(1, -1), "w_k": w_k.astype(BF16), "w_v": w_v.astype(BF16),
        "w_pool": w_pool[l].astype(BF16), "pool_scale": pool_scale[l].reshape(1, -1),
        "attn_out_norm": attn_out_norm[l].reshape(1, -1),
        "pool_out_norm": pool_out_norm[l].reshape(1, -1),
        "w_out": w_out[l].astype(BF16), "ln2": ln2[l].reshape(1, -1),
        "w_r_hi": wr_hi, "w_r_lo": wr_lo, "b_r": br,
        "w_gate": w_gate[l].astype(BF16), "b_gate": b_gate[l].reshape(N_EXPERTS, 1, -1),
        "w_up": w_up[l].astype(BF16), "b_up": b_up[l].reshape(N_EXPERTS, 1, -1),
        "w_down": w_down[l].astype(BF16), "b_down": b_down[l].reshape(N_EXPERTS, 1, -1),
    }


def _tiles(seq_len):
    tm = min(512, seq_len)
    tq = min(512, seq_len)
    n_sub = min(2048, seq_len) // tq
    tk = min(2048, seq_len)
    bm = 512
    return tm, tq, n_sub, tk, bm


def _trunk(x3, layers, final_norm):
    B, S, _ = x3.shape
    T = B * S
    tm, tq, n_sub, tk, bm = _tiles(S)
    tabs = _rope_tables(S)
    x = x3.reshape(T, D_MODEL)
    fn = final_norm.reshape(1, -1)
    for li, lw in enumerate(layers):
        q, k, v, u = _proj_call(x, lw, tabs, S, tm)
        o = _attn_call(q, k, v, B, S, tq, tk, n_sub)
        x, h2, idx_pad, gate_pad, rank_pad, tile_cnt = _mix_call(x, o, u, lw, S, tm)
        chunk_src, comb_src, block_expert, n_used = _routing_tables(tile_cnt, tm, bm)
        xl = _localsort_call(h2, idx_pad, rank_pad, tile_cnt, tm)
        y = _ffn_call(xl, chunk_src, block_expert, n_used, lw, bm)
        x = _combine_call(x, idx_pad, rank_pad, gate_pad, tile_cnt, comb_src, y, fn, tm,
                          final=(li == len(layers) - 1))
    return x.reshape(B, S, D_MODEL)


def kernel(x_prompt, x_sample, ln1, w_in, q_norm, w_uq, kv_norm, w_ukv, w_pool, pool_scale,
           attn_out_norm, pool_out_norm, w_out, ln2, w_router, b_router, w_gate, b_gate, w_up,
           b_up, w_down, b_down, final_norm):
    params = (ln1, w_in, q_norm, w_uq, kv_norm, w_ukv, w_pool, pool_scale, attn_out_norm,
              pool_out_norm, w_out, ln2, w_router, b_router, w_gate, b_gate, w_up, b_up,
              w_down, b_down)
    layers = [_prep_layer(l, *params) for l in range(ln1.shape[0])]
    return (_trunk(x_prompt, layers, final_norm), _trunk(x_sample, layers, final_norm))
```

```python
import functools

import jax
import jax.numpy as jnp
import numpy as np
from jax import lax
from jax.experimental import pallas as pl
from jax.experimental.pallas import tpu as pltpu

F32 = jnp.float32
BF16 = jnp.bfloat16
I32 = jnp.int32

D_MODEL = 1024
N_HEADS = 8
HEAD_NOPE = 64
HEAD_ROPE = 32
HEAD_V = 64
HEAD_PAD = 128
Q_LORA = 256
KV_LORA = 128
ATT_W = 512
POOL_W = 512
POOL_WINDOWS = (2, 4, 8, 16)
POOL_GW = 128
POOL_HALO = 8
ROPE_THETA = 10000.0
N_EXPERTS = 32
TOP_K = 4
SWIGLU_LIMIT = 7.0
SWIGLU_ALPHA = 1.702
NORM_EPS = 1e-5
ROUTER_PAD = 128
SUBLANES = 8

_C_Q = 0
_C_KV = _C_Q + Q_LORA
_C_KRA = _C_KV + KV_LORA
_C_KRB = _C_KRA + HEAD_PAD
_C_U = _C_KRB + HEAD_PAD
IN_W_EXT = _C_U + POOL_W

VMEM_LIMIT = 56 * 1024 * 1024


def _cparams(n_axes):
    return pltpu.CompilerParams(dimension_semantics=("arbitrary",) * n_axes,
                                vmem_limit_bytes=VMEM_LIMIT)


def _rms(x, g):
    return x * lax.rsqrt(jnp.mean(x * x, axis=-1, keepdims=True) + NORM_EPS) * g


def _proj_kernel(x_ref, ln1_ref, win_ref, qn_ref, wqa_ref, wqb_ref, kvn_ref, wk_ref, wv_ref,
                 cq_ref, sq_ref, ck_ref, sk_ref, q_ref, k_ref, v_ref, u_ref):
    h = _rms(x_ref[...], ln1_ref[...]).astype(BF16)
    proj = jnp.dot(h, win_ref[...], preferred_element_type=F32)
    u_ref[...] = proj[:, _C_U:_C_U + POOL_W]
    hq = _rms(proj[:, _C_Q:_C_Q + Q_LORA], qn_ref[...]).astype(BF16)
    qa = jnp.dot(hq, wqa_ref[...], preferred_element_type=F32)
    qb = jnp.dot(hq, wqb_ref[...], preferred_element_type=F32)
    hkv = _rms(proj[:, _C_KV:_C_KV + KV_LORA], kvn_ref[...]).astype(BF16)
    kn = jnp.dot(hkv, wk_ref[...], preferred_element_type=F32)
    vv = jnp.dot(hkv, wv_ref[...], preferred_element_type=F32)
    lane = lax.broadcasted_iota(I32, vv.shape, 1)
    v_ref[...] = jnp.where((lane & (HEAD_PAD - 1)) == HEAD_V, 1.0, vv).astype(BF16)
    k_rope = (proj[:, _C_KRA:_C_KRA + HEAD_PAD] * ck_ref[...]
              + proj[:, _C_KRB:_C_KRB + HEAD_PAD] * sk_ref[...])
    cq = cq_ref[...]
    sq = sq_ref[...]
    for hd in range(N_HEADS):
        sl = slice(hd * HEAD_PAD, (hd + 1) * HEAD_PAD)
        q_ref[:, sl] = (qa[:, sl] * cq + qb[:, sl] * sq).astype(BF16)
        k_ref[:, sl] = (kn[:, sl] + k_rope).astype(BF16)


def _proj_call(x, lw, tabs, seq_len, tm):
    T = x.shape[0]
    tiles_per_seq = seq_len // tm
    full = lambda shape: pl.BlockSpec(shape, lambda i: (0,) * len(shape))
    tab = pl.BlockSpec((tm, HEAD_PAD), lambda i: (i % tiles_per_seq, 0))
    row = lambda w: pl.BlockSpec((tm, w), lambda i: (i, 0))
    return pl.pallas_call(
        _proj_kernel,
        grid=(T // tm,),
        in_specs=[row(D_MODEL), full((1, D_MODEL)), full((D_MODEL, IN_W_EXT)),
                  full((1, Q_LORA)), full((Q_LORA, N_HEADS * HEAD_PAD)),
                  full((Q_LORA, N_HEADS * HEAD_PAD)),
                  full((1, KV_LORA)), full((KV_LORA, N_HEADS * HEAD_PAD)),
                  full((KV_LORA, N_HEADS * HEAD_PAD)), tab, tab, tab, tab],
        out_specs=[row(N_HEADS * HEAD_PAD), row(N_HEADS * HEAD_PAD), row(N_HEADS * HEAD_PAD),
                   row(POOL_W)],
        out_shape=[jax.ShapeDtypeStruct((T, N_HEADS * HEAD_PAD), BF16),
                   jax.ShapeDtypeStruct((T, N_HEADS * HEAD_PAD), BF16),
                   jax.ShapeDtypeStruct((T, N_HEADS * HEAD_PAD), BF16),
                   jax.ShapeDtypeStruct((T, POOL_W), F32)],
        compiler_params=_cparams(1),
        name="proj",
    )(x, lw["ln1"], lw["w_in"], lw["q_norm"], lw["w_qa"], lw["w_qb"], lw["kv_norm"],
      lw["w_k"], lw["w_v"], tabs["cq"], tabs["sq"], tabs["ck"], tabs["sk"])


def _attn_kernel(q_ref, k_ref, v_ref, o_ref, m_scr, acc_scr, sa_scr, *, tq, tk, n_sub, n_chunks):
    m_scr[...] = jnp.full(m_scr.shape, -jnp.inf, F32)
    acc_scr[...] = jnp.zeros(acc_scr.shape, F32)
    n_steps = n_sub * n_chunks

    def head_cols(hh):
        return slice(hh * HEAD_PAD, (hh + 1) * HEAD_PAD)

    def block(i, size):
        return pl.ds(i * size if isinstance(i, int) else pl.multiple_of(i * size, size), size)

    def split(n):
        return (n // n_chunks, n % n_chunks) if isinstance(n, int) else (
            lax.div(n, n_chunks), lax.rem(n, n_chunks))

    def scores(hh, n):
        sub, j = split(n)
        kc = k_ref[block(j, tk), head_cols(hh)]
        return lax.dot_general(q_ref[block(sub, tq), head_cols(hh)], kc,
                               (((1,), (1,)), ((), ())),
                               preferred_element_type=F32)

    def update(hh, s, n):
        sub, j = split(n)
        vc = v_ref[block(j, tk), head_cols(hh)]
        m_old = m_scr[sub, hh]
        m_new = jnp.maximum(m_old, jnp.max(s, axis=-1, keepdims=True))
        alpha = jnp.exp2(m_old - m_new)
        p = jnp.exp2(s - m_new).astype(BF16)
        acc_scr[sub, hh] = alpha * acc_scr[sub, hh] + jnp.dot(p, vc, preferred_element_type=F32)
        m_scr[sub, hh] = m_new

    sa_scr[...] = scores(0, 0)

    def step(n, carry):
        sb = scores(1, n)
        update(0, sa_scr[...], n)
        sa_scr[...] = scores(0, n + 1)
        update(1, sb, n)
        return carry

    lax.fori_loop(0, n_steps - 1, step, 0, unroll=2)
    sb = scores(1, n_steps - 1)
    update(0, sa_scr[...], n_steps - 1)
    update(1, sb, n_steps - 1)
    for sub in range(n_sub):
        outs = []
        for hh in range(2):
            acc = acc_scr[sub, hh]
            outs.append(acc[:, :HEAD_V] / acc[:, HEAD_V:HEAD_V + 1])
        o_ref[sub * tq:(sub + 1) * tq, :] = jnp.concatenate(outs, axis=-1).astype(o_ref.dtype)


def _attn_call(q, k, v, batch, seq_len, tq, tk, n_sub):
    T = q.shape[0]
    rows = tq * n_sub
    n_q = seq_len // rows
    kern = functools.partial(_attn_kernel, tq=tq, tk=tk, n_sub=n_sub, n_chunks=seq_len // tk)
    pair = lambda r: pl.BlockSpec((r, 2 * HEAD_PAD), lambda b, hp, i: (b, hp),
                                  pipeline_mode=pl.Buffered(1))
    return pl.pallas_call(
        kern,
        grid=(batch, N_HEADS // 2, n_q),
        in_specs=[pl.BlockSpec((rows, 2 * HEAD_PAD), lambda b, hp, i: (b * n_q + i, hp)),
                  pair(seq_len), pair(seq_len)],
        out_specs=pl.BlockSpec((rows, 2 * HEAD_V), lambda b, hp, i: (b * n_q + i, hp)),
        out_shape=jax.ShapeDtypeStruct((T, ATT_W), BF16),
        scratch_shapes=[pltpu.VMEM((n_sub, 2, tq, 1), F32),
                        pltpu.VMEM((n_sub, 2, tq, HEAD_PAD), F32),
                        pltpu.VMEM((tq, tk), F32)],
        compiler_params=_cparams(3),
        name="attn",
    )(q, k, v)


def _mix_kernel(x_ref, o_ref, u_ref, up_ref, un_ref, an_ref, pn_ref, wpool_ref, pscale_ref,
                wout_ref, ln2_ref, wrc_ref, br_ref,
                xn_ref, h2_ref, idx_ref, gate_ref, rank_ref, cnt_ref, ubuf, *, tm, seq_len):
    i = pl.program_id(0)
    tiles_per_seq = seq_len // tm
    ti = i % tiles_per_seq
    prev_ok = jnp.where(ti > 0, 1.0, 0.0).astype(F32)
    next_ok = jnp.where(ti < tiles_per_seq - 1, 1.0, 0.0).astype(F32)
    ubuf[0:POOL_HALO, :] = up_ref[...] * prev_ok
    ubuf[POOL_HALO:POOL_HALO + tm, :] = u_ref[...]
    ubuf[POOL_HALO + tm:2 * POOL_HALO + tm, :] = un_ref[...] * next_ok

    pos = ti * tm + lax.broadcasted_iota(I32, (tm, 1), 0)
    pooled = []
    for g, w in enumerate(POOL_WINDOWS):
        cols = slice(g * POOL_GW, (g + 1) * POOL_GW)
        tot = jnp.zeros((tm, POOL_GW), F32)
        for d in range(-(w // 2), w - w // 2):
            tot = tot + ubuf[POOL_HALO + d:POOL_HALO + d + tm, cols]
        lo = jnp.maximum(pos - w // 2, 0)
        hi = jnp.minimum(pos + (w - w // 2), seq_len)
        cnt = (hi - lo).astype(F32)
        pg = (tot / cnt - u_ref[:, cols]).astype(BF16)
        yg = jnp.dot(pg, wpool_ref[g], preferred_element_type=F32) * pscale_ref[:, cols]
        pooled.append(yg)
    pool = jnp.concatenate(pooled, axis=-1)
    attn = o_ref[...].astype(F32)
    mixed = jnp.concatenate([_rms(attn, an_ref[...]), _rms(pool, pn_ref[...])],
                            axis=-1).astype(BF16)
    xn = x_ref[...] + jnp.dot(mixed, wout_ref[...], preferred_element_type=F32)
    xn_ref[...] = xn
    h2 = _rms(xn, ln2_ref[...])
    hi_ = h2.astype(BF16)
    h2_ref[...] = hi_
    lo_ = (h2 - hi_.astype(F32)).astype(BF16)
    hi_w = jnp.dot(hi_, wrc_ref[...], preferred_element_type=F32)
    logits = (hi_w[:, :ROUTER_PAD] + hi_w[:, ROUTER_PAD:]
              + jnp.dot(lo_, wrc_ref[:, :ROUTER_PAD], preferred_element_type=F32)) + br_ref[...]
    lane = lax.broadcasted_iota(I32, logits.shape, 1)
    lane_f = lane.astype(F32)
    vals, idxs = [], []
    for _ in range(TOP_K):
        mx = jnp.max(logits, axis=-1, keepdims=True)
        ix = jnp.min(jnp.where(logits == mx, lane_f, float(ROUTER_PAD)), axis=-1,
                     keepdims=True).astype(I32)
        vals.append(mx)
        idxs.append(ix)
        logits = jnp.where(lane == ix, -jnp.inf, logits)
    es = [jnp.exp(vk - vals[0]) for vk in vals]
    den = es[0] + es[1] + es[2] + es[3]
    hits = [lane == idxs[kk] for kk in range(TOP_K)]
    routed = jnp.where(hits[0] | hits[1] | hits[2] | hits[3], 1.0, 0.0)
    r_i = lax.broadcasted_iota(I32, (tm, tm), 0)
    c_i = lax.broadcasted_iota(I32, (tm, tm), 1)
    ltri = jnp.where(c_i < r_i, 1.0, 0.0).astype(BF16)
    before = jnp.dot(ltri, routed.astype(BF16), preferred_element_type=F32)
    idx_out = jnp.zeros(lane.shape, I32)
    gate_out = jnp.zeros(lane.shape, F32)
    rank_out = jnp.zeros(lane.shape, I32)
    for kk in range(TOP_K):
        rk = jnp.sum(jnp.where(hits[kk], before, 0.0), axis=-1, keepdims=True).astype(I32)
        idx_out = jnp.where(lane == kk, idxs[kk], idx_out)
        gate_out = jnp.where(lane == kk, es[kk] / den, gate_out)
        rank_out = jnp.where(lane == kk, rk, rank_out)
    idx_ref[...] = idx_out
    gate_ref[...] = gate_out
    rank_ref[...] = rank_out
    counts = jnp.sum(routed, axis=0, keepdims=True).astype(I32)
    cnt_ref[0] = jnp.broadcast_to(counts, (SUBLANES, ROUTER_PAD))


def _mix_call(x, o, u, lw, seq_len, tm):
    T = x.shape[0]
    hb = tm // POOL_HALO
    n_hblk = T // POOL_HALO
    full = lambda shape: pl.BlockSpec(shape, lambda i: (0,) * len(shape))
    row = lambda w: pl.BlockSpec((tm, w), lambda i: (i, 0))
    kern = functools.partial(_mix_kernel, tm=tm, seq_len=seq_len)
    return pl.pallas_call(
        kern,
        grid=(T // tm,),
        in_specs=[row(D_MODEL), row(ATT_W), row(POOL_W),
                  pl.BlockSpec((POOL_HALO, POOL_W), lambda i: (jnp.maximum(i * hb - 1, 0), 0)),
                  pl.BlockSpec((POOL_HALO, POOL_W),
                               lambda i: (jnp.minimum((i + 1) * hb, n_hblk - 1), 0)),
                  full((1, ATT_W)), full((1, POOL_W)), full((4, POOL_GW, POOL_GW)),
                  full((1, POOL_W)), full((ATT_W + POOL_W, D_MODEL)), full((1, D_MODEL)),
                  full((D_MODEL, 2 * ROUTER_PAD)),
                  full((1, ROUTER_PAD))],
        out_specs=[row(D_MODEL), row(D_MODEL), row(ROUTER_PAD), row(ROUTER_PAD), row(ROUTER_PAD),
                   pl.BlockSpec((1, SUBLANES, ROUTER_PAD), lambda i: (i, 0, 0))],
        out_shape=[jax.ShapeDtypeStruct((T, D_MODEL), F32),
                   jax.ShapeDtypeStruct((T, D_MODEL), BF16),
                   jax.ShapeDtypeStruct((T, ROUTER_PAD), I32),
                   jax.ShapeDtypeStruct((T, ROUTER_PAD), F32),
                   jax.ShapeDtypeStruct((T, ROUTER_PAD), I32),
                   jax.ShapeDtypeStruct((T // tm, SUBLANES, ROUTER_PAD), I32)],
        scratch_shapes=[pltpu.VMEM((tm + 2 * POOL_HALO, POOL_W), F32)],
        compiler_params=_cparams(1),
        name="mix",
    )(x, o, u, u, u, lw["attn_out_norm"], lw["pool_out_norm"], lw["w_pool"], lw["pool_scale"],
      lw["w_out"], lw["ln2"], lw["w_r"], lw["b_r"])


def _run_starts(cnt_tile):
    n_ch_e = ((cnt_tile + SUBLANES - 1) // SUBLANES).astype(F32)
    e_r = lax.broadcasted_iota(I32, (ROUTER_PAD, ROUTER_PAD), 0)
    e_c = lax.broadcasted_iota(I32, (ROUTER_PAD, ROUTER_PAD), 1)
    before = jnp.where(e_r < e_c, 1.0, 0.0).astype(BF16)
    starts = jnp.dot(n_ch_e.astype(BF16), before, preferred_element_type=F32)[0:1, :]
    return starts * float(SUBLANES)


def _local_rows(idx, rank, run_start, tm):
    lane = lax.broadcasted_iota(I32, (tm, ROUTER_PAD), 1)
    rows = []
    for kk in range(TOP_K):
        e_k = jnp.sum(jnp.where(lane == kk, idx, 0), axis=-1, keepdims=True)
        start_k = jnp.sum(jnp.where(lane == e_k, run_start, 0.0), axis=-1, keepdims=True)
        rank_k = jnp.sum(jnp.where(lane == kk, rank, 0), axis=-1, keepdims=True)
        rows.append(start_k.astype(I32) + rank_k)
    return rows


def _localsort_kernel(h_ref, idx_ref, rank_ref, cnt_ref, xl_ref, *, tm, lrows):
    rows = _local_rows(idx_ref[...], rank_ref[...], _run_starts(cnt_ref[0]), tm)
    lane = lax.broadcasted_iota(I32, (tm, ROUTER_PAD), 1)
    rows_pad = jnp.full((tm, ROUTER_PAD), -1.0, F32)
    for kk in range(TOP_K):
        rows_pad = jnp.where(lane == kk, rows[kk].astype(F32), rows_pad)
    rows_t = rows_pad.T
    r_iota = lax.broadcasted_iota(I32, (lrows, tm), 0).astype(F32)
    sel = jnp.zeros((lrows, tm), F32)
    for kk in range(TOP_K):
        sel = jnp.where(r_iota == rows_t[kk:kk + 1, :], 1.0, sel)
    xl_ref[...] = jnp.dot(sel.astype(BF16), h_ref[...], preferred_element_type=F32)


def _localsort_call(h2, idx_pad, rank_pad, tile_cnt, tm):
    T = h2.shape[0]
    n_tiles = T // tm
    lrows = _local_buffer_rows(tm)
    row = lambda w: pl.BlockSpec((tm, w), lambda i: (i, 0))
    return pl.pallas_call(
        functools.partial(_localsort_kernel, tm=tm, lrows=lrows),
        grid=(n_tiles,),
        in_specs=[row(D_MODEL), row(ROUTER_PAD), row(ROUTER_PAD),
                  pl.BlockSpec((1, SUBLANES, ROUTER_PAD), lambda i: (i, 0, 0))],
        out_specs=pl.BlockSpec((lrows, D_MODEL), lambda i: (i, 0)),
        out_shape=jax.ShapeDtypeStruct((n_tiles * lrows, D_MODEL), F32),
        compiler_params=_cparams(1),
        name="localsort",
    )(h2, idx_pad, rank_pad, tile_cnt)


def _ffn_kernel(be_ref, nu_ref, src0_ref, srcn_ref, xl_ref, wg_ref, bg_ref, wu_ref, bu_ref,
                wd_ref, bd_ref, y_ref, xbuf, sem, *, bm, n_blocks):
    del be_ref
    i = pl.program_id(0)
    nu = nu_ref[0]
    slot = i % 2
    n_ch = bm // SUBLANES

    def chunk_copy(src_ref, s, c):
        src_row = pl.multiple_of(src_ref[0, 0, c], SUBLANES)
        return pltpu.make_async_copy(xl_ref.at[pl.ds(src_row, SUBLANES)],
                                     xbuf.at[s, pl.ds(c * SUBLANES, SUBLANES)], sem.at[s])

    def block_wait(s):
        pltpu.make_async_copy(xl_ref.at[pl.ds(0, bm)], xbuf.at[s], sem.at[s]).wait()

    @pl.when(i == 0)
    def _():
        for c in range(n_ch):
            chunk_copy(src0_ref, 0, c).start()

    @pl.when(i < nu)
    def _():
        for c in range(n_ch):
            chunk_copy(srcn_ref, 1 - slot, c).start()
        block_wait(slot)
        x = xbuf[slot].astype(BF16)
        gt = jnp.minimum(jnp.dot(x, wg_ref[0], preferred_element_type=F32) + bg_ref[0],
                         SWIGLU_LIMIT)
        up = jnp.clip(jnp.dot(x, wu_ref[0], preferred_element_type=F32) + bu_ref[0],
                      -SWIGLU_LIMIT, SWIGLU_LIMIT)
        act = gt * jax.nn.sigmoid(SWIGLU_ALPHA * gt) * (up + 1.0)
        y_ref[...] = jnp.dot(act.astype(BF16), wd_ref[0], preferred_element_type=F32) + bd_ref[0]

    @pl.when(i >= nu)
    def _():
        y_ref[...] = jnp.zeros(y_ref.shape, y_ref.dtype)

    @pl.when(i == n_blocks - 1)
    def _():
        block_wait(nu % 2)


def _ffn_call(xl, chunk_src, block_expert, n_used, lw, bm):
    n_blocks = chunk_src.shape[0]
    wmap = lambda i, be, nu: (be[i], 0, 0)
    stab = lambda imap: pl.BlockSpec((1, 1, ROUTER_PAD), imap, memory_space=pltpu.SMEM)
    grid_spec = pltpu.PrefetchScalarGridSpec(
        num_scalar_prefetch=2,
        grid=(n_blocks,),
        in_specs=[stab(lambda i, be, nu: (0, 0, 0)),
                  stab(lambda i, be, nu: (jnp.minimum(i + 1, n_blocks - 1), 0, 0)),
                  pl.BlockSpec(memory_space=pl.ANY),
                  pl.BlockSpec((1, D_MODEL, D_MODEL), wmap), pl.BlockSpec((1, 1, D_MODEL), wmap),
                  pl.BlockSpec((1, D_MODEL, D_MODEL), wmap), pl.BlockSpec((1, 1, D_MODEL), wmap),
                  pl.BlockSpec((1, D_MODEL, D_MODEL), wmap), pl.BlockSpec((1, 1, D_MODEL), wmap)],
        out_specs=pl.BlockSpec((bm, D_MODEL), lambda i, be, nu: (i, 0)),
        scratch_shapes=[pltpu.VMEM((2, bm, D_MODEL), F32), pltpu.SemaphoreType.DMA((2,))],
    )
    return pl.pallas_call(
        functools.partial(_ffn_kernel, bm=bm, n_blocks=n_blocks),
        grid_spec=grid_spec,
        out_shape=jax.ShapeDtypeStruct((n_blocks * bm, D_MODEL), F32),
        compiler_params=_cparams(1),
        name="ffn",
    )(block_expert, n_used, chunk_src, chunk_src, xl, lw["w_gate"], lw["b_gate"], lw["w_up"],
      lw["b_up"], lw["w_down"], lw["b_down"])


def _combine_kernel(src_ref, srcn_ref, x_ref, idx_ref, rank_ref, gate_ref, cnt_ref, fn_ref, y_ref,
                    out_ref, ylocal, sem, *, tm, lrows, n_tiles, final):
    i = pl.program_id(0)
    slot = i % 2
    n_ch = lrows // SUBLANES

    def request(table, s):
        def per_chunk(c, carry):
            src_row = pl.multiple_of(table[0, 0, c], SUBLANES)
            dst_row = pl.multiple_of(c * SUBLANES, SUBLANES)
            pltpu.make_async_copy(y_ref.at[pl.ds(src_row, SUBLANES)],
                                  ylocal.at[s, pl.ds(dst_row, SUBLANES)], sem.at[s]).start()
            return carry

        lax.fori_loop(0, n_ch, per_chunk, 0, unroll=8)

    @pl.when(i == 0)
    def _():
        request(src_ref, 0)

    @pl.when(i + 1 < n_tiles)
    def _():
        request(srcn_ref, 1 - slot)

    pltpu.make_async_copy(y_ref.at[pl.ds(0, lrows)], ylocal.at[slot], sem.at[slot]).wait()

    rows = _local_rows(idx_ref[...], rank_ref[...], _run_starts(cnt_ref[0]), tm)
    lane = lax.broadcasted_iota(I32, (tm, ROUTER_PAD), 1)
    gates = gate_ref[...]
    r_iota = lax.broadcasted_iota(I32, (tm, lrows), 1)
    gmat = jnp.zeros((tm, lrows), F32)
    for kk in range(TOP_K):
        g_k = jnp.sum(jnp.where(lane == kk, gates, 0.0), axis=-1, keepdims=True)
        gmat = jnp.where(r_iota == rows[kk], g_k, gmat)
    moe = jnp.dot(gmat.astype(BF16), ylocal[slot].astype(BF16), preferred_element_type=F32)
    acc = x_ref[...] + moe
    if final:
        acc = _rms(acc, fn_ref[...])
    out_ref[...] = acc


def _combine_call(x, idx_pad, rank_pad, gate_pad, tile_cnt, comb_src, y, final_norm, tm, final):
    T = x.shape[0]
    n_tiles = T // tm
    lrows = _local_buffer_rows(tm)
    kern = functools.partial(_combine_kernel, tm=tm, lrows=lrows, n_tiles=n_tiles, final=final)
    stab = lambda imap: pl.BlockSpec((1, 1, comb_src.shape[-1]), imap, memory_space=pltpu.SMEM)
    row = lambda w: pl.BlockSpec((tm, w), lambda i: (i, 0))
    return pl.pallas_call(
        kern,
        grid=(n_tiles,),
        in_specs=[stab(lambda i: (i, 0, 0)),
                  stab(lambda i: (jnp.minimum(i + 1, n_tiles - 1), 0, 0)),
                  row(D_MODEL), row(ROUTER_PAD), row(ROUTER_PAD), row(ROUTER_PAD),
                  pl.BlockSpec((1, SUBLANES, ROUTER_PAD), lambda i: (i, 0, 0)),
                  pl.BlockSpec((1, D_MODEL), lambda i: (0, 0)),
                  pl.BlockSpec(memory_space=pl.ANY)],
        out_specs=row(D_MODEL),
        out_shape=jax.ShapeDtypeStruct((T, D_MODEL), F32),
        scratch_shapes=[pltpu.VMEM((2, lrows, D_MODEL), F32), pltpu.SemaphoreType.DMA((2,))],
        compiler_params=_cparams(1),
        name="combine",
    )(comb_src, comb_src, x, idx_pad, rank_pad, gate_pad, tile_cnt, final_norm, y)


def _local_buffer_rows(tm):
    return tm * TOP_K + N_EXPERTS * SUBLANES


def _routing_tables(tile_cnt, tm, bm):
    n_tiles = tile_cnt.shape[0]
    lrows = _local_buffer_rows(tm)
    ch_blk = bm // SUBLANES
    cnt = tile_cnt[:, 0, :N_EXPERTS]
    n_ch = (cnt + SUBLANES - 1) // SUBLANES
    reg = jnp.sum(n_ch, axis=0)
    reg_pad = (reg + ch_blk - 1) // ch_blk * ch_blk
    cend = jnp.cumsum(reg_pad)
    cstart = cend - reg_pad
    before_tile = jnp.cumsum(n_ch, axis=0) - n_ch
    gstart = ((cstart[None, :] + before_tile) * SUBLANES).astype(I32)
    lstart = jnp.cumsum(n_ch, axis=1) - n_ch
    max_rows = n_tiles * (tm * TOP_K + N_EXPERTS * (SUBLANES - 1))
    n_blocks = (max_rows + bm - 1) // bm + N_EXPERTS
    blk_first = jnp.arange(n_blocks, dtype=I32) * ch_blk
    block_expert = jnp.minimum(jnp.sum(cend[None, :] <= blk_first[:, None], axis=1),
                               N_EXPERTS - 1).astype(I32)
    n_used = (cend[-1] // ch_blk).astype(I32).reshape(1)
    mine = block_expert[:, None] == jnp.arange(N_EXPERTS, dtype=I32)[None, :]
    pick = lambda tab: jnp.sum(jnp.where(mine[:, None, :], tab[None, :, :], 0), axis=-1)
    cstart_b = jnp.sum(jnp.where(mine, cstart[None, :], 0), axis=-1)
    reg_b = jnp.sum(jnp.where(mine, reg[None, :], 0), axis=-1)
    before_b = pick(before_tile)
    ends_b = pick(before_tile + n_ch)
    lstart_b = pick(lstart)
    o = (blk_first - cstart_b)[:, None] + jnp.arange(ch_blk, dtype=I32)[None, :]
    tile_c = jnp.minimum(jnp.sum(ends_b[:, None, :] <= o[:, :, None], axis=-1), n_tiles - 1)
    at_tile = tile_c[:, :, None] == jnp.arange(n_tiles, dtype=I32)[None, None, :]
    local = jnp.sum(jnp.where(at_tile, (lstart_b - before_b)[:, None, :], 0), axis=-1)
    src = tile_c * (lrows // SUBLANES) + local + o
    src = jnp.where(o < reg_b[:, None], src, lrows // SUBLANES - 1) * SUBLANES
    chunk_src = jnp.pad(src.astype(I32),
                        ((0, 0), (0, ROUTER_PAD - ch_blk))).reshape(n_blocks, 1, ROUTER_PAD)
    lc = jnp.arange(lrows // SUBLANES, dtype=I32)[None, :, None]
    in_run = (lstart[:, None, :] <= lc) & (lc < (lstart + n_ch)[:, None, :])
    back = jnp.sum(jnp.where(in_run, gstart[:, None, :] + (lc - lstart[:, None, :]) * SUBLANES,
                             0), axis=-1).astype(I32)
    lpad = -(lrows // SUBLANES) % ROUTER_PAD
    comb_src = jnp.pad(back, ((0, 0), (0, lpad))).reshape(n_tiles, 1, -1)
    return chunk_src, comb_src, block_expert, n_used


def _rope_tables(seq_len):
    inv = 1.0 / (ROPE_THETA ** (jnp.arange(0, HEAD_ROPE, 2, dtype=F32) / HEAD_ROPE))
    ang = jnp.arange(seq_len, dtype=F32)[:, None] * inv[None, :]
    cos, sin = jnp.cos(ang), jnp.sin(ang)
    scale = np.float32(np.log2(np.e) / np.sqrt(HEAD_NOPE + HEAD_ROPE))
    z = jnp.zeros((seq_len, HEAD_PAD - HEAD_NOPE - HEAD_ROPE), F32)
    ones = jnp.ones((seq_len, HEAD_NOPE), F32)
    zeros = jnp.zeros((seq_len, HEAD_NOPE), F32)
    ck = jnp.concatenate([zeros, cos, cos, z], axis=1)
    sk = jnp.concatenate([zeros, sin, sin, z], axis=1)
    cq = jnp.concatenate([ones, cos, cos, z], axis=1) * scale
    sq = sk * scale
    return {"cq": cq, "sq": sq, "ck": ck, "sk": sk}


def _pad_heads(w_nope, w_r1, w_r2):
    lead = w_nope.shape[:-2]
    z = jnp.zeros(lead + (N_HEADS, HEAD_PAD - HEAD_NOPE - HEAD_ROPE), w_nope.dtype)
    return jnp.concatenate([w_nope, w_r1, w_r2, z], axis=-1).reshape(lead + (N_HEADS * HEAD_PAD,))


def _prep_layer(l, ln1, w_in, q_norm, w_uq, kv_norm, w_ukv, w_pool, pool_scale, attn_out_norm,
                pool_out_norm, w_out, ln2, w_router, b_router, w_gate, b_gate, w_up, b_up,
                w_down, b_down):
    half = HEAD_ROPE // 2
    wi = w_in[l]
    kr = wi[:, Q_LORA + KV_LORA:Q_LORA + KV_LORA + HEAD_ROPE]
    kr1, kr2 = kr[:, :half], kr[:, half:]
    zn = jnp.zeros((D_MODEL, HEAD_NOPE), F32)
    zp = jnp.zeros((D_MODEL, HEAD_PAD - HEAD_NOPE - HEAD_ROPE), F32)
    kra = jnp.concatenate([zn, kr1, kr2, zp], axis=1)
    krb = jnp.concatenate([zn, -kr2, kr1, zp], axis=1)
    w_in_ext = jnp.concatenate(
        [wi[:, :Q_LORA + KV_LORA], kra, krb, wi[:, Q_LORA + KV_LORA + HEAD_ROPE:]], axis=1)
    wq = w_uq[l].reshape(Q_LORA, N_HEADS, HEAD_NOPE + HEAD_ROPE)
    qn_, q1, q2 = wq[..., :HEAD_NOPE], wq[..., HEAD_NOPE:HEAD_NOPE + half], wq[..., HEAD_NOPE + half:]
    w_qa = _pad_heads(qn_, q1, q2)
    w_qb = _pad_heads(jnp.zeros_like(qn_), -q2, q1)
    wkv = w_ukv[l].reshape(KV_LORA, N_HEADS, HEAD_NOPE + HEAD_V)
    kz = jnp.zeros((KV_LORA, N_HEADS, half), F32)
    w_k = _pad_heads(wkv[..., :HEAD_NOPE], kz, kz)
    w_v = jnp.pad(wkv[..., HEAD_NOPE:], ((0, 0), (0, 0), (0, HEAD_PAD - HEAD_V))).reshape(
        KV_LORA, N_HEADS * HEAD_PAD)
    wr = jnp.pad(w_router[l], ((0, 0), (0, ROUTER_PAD - N_EXPERTS)))
    wr_hi = wr.astype(BF16)
    wr_lo = (wr - wr_hi.astype(F32)).astype(BF16)
    br = jnp.concatenate([b_router[l].astype(F32),
                          jnp.full((ROUTER_PAD - N_EXPERTS,), -jnp.inf, F32)]).reshape(1, ROUTER_PAD)
    return {
        "ln1": ln1[l].reshape(1, -1), "w_in": w_in_ext.astype(BF16),
        "q_norm": q_norm[l].reshape(1, -1), "w_qa": w_qa.astype(BF16), "w_qb": w_qb.astype(BF16),
        "kv_norm": kv_norm[l].reshape(1, -1), "w_k": w_k.astype(BF16), "w_v": w_v.astype(BF16),
        "w_pool": w_pool[l].astype(BF16), "pool_scale": pool_scale[l].reshape(1, -1),
        "attn_out_norm": attn_out_norm[l].reshape(1, -1),
        "pool_out_norm": pool_out_norm[l].reshape(1, -1),
        "w_out": w_out[l].astype(BF16), "ln2": ln2[l].reshape(1, -1),
        "w_r": jnp.concatenate([wr_hi, wr_lo], axis=1), "b_r": br,
        "w_gate": w_gate[l].astype(BF16), "b_gate": b_gate[l].reshape(N_EXPERTS, 1, -1),
        "w_up": w_up[l].astype(BF16), "b_up": b_up[l].reshape(N_EXPERTS, 1, -1),
        "w_down": w_down[l].astype(BF16), "b_down": b_down[l].reshape(N_EXPERTS, 1, -1),
    }


def _tiles(seq_len):
    tm = min(512, seq_len)
    tq = min(512, seq_len)
    n_sub = min(2048, seq_len) // tq
    tk = min(2048, seq_len)
    bm = 512
    return tm, tq, n_sub, tk, bm


def _trunk(x3, layers, final_norm):
    B, S, _ = x3.shape
    T = B * S
    tm, tq, n_sub, tk, bm = _tiles(S)
    tabs = _rope_tables(S)
    x = x3.reshape(T, D_MODEL)
    fn = final_norm.reshape(1, -1)
    for li, lw in enumerate(layers):
        q, k, v, u = _proj_call(x, lw, tabs, S, tm)
        o = _attn_call(q, k, v, B, S, tq, tk, n_sub)
        x, h2, idx_pad, gate_pad, rank_pad, tile_cnt = _mix_call(x, o, u, lw, S, tm)
        chunk_src, comb_src, block_expert, n_used = _routing_tables(tile_cnt, tm, bm)
        xl = _localsort_call(h2, idx_pad, rank_pad, tile_cnt, tm)
        y = _ffn_call(xl, chunk_src, block_expert, n_used, lw, bm)
        x = _combine_call(x, idx_pad, rank_pad, gate_pad, tile_cnt, comb_src, y, fn, tm,
                          final=(li == len(layers) - 1))
    return x.reshape(B, S, D_MODEL)


def kernel(x_prompt, x_sample, ln1, w_in, q_norm, w_uq, kv_norm, w_ukv, w_pool, pool_scale,
           attn_out_norm, pool_out_norm, w_out, ln2, w_router, b_router, w_gate, b_gate, w_up,
           b_up, w_down, b_down, final_norm):
    params = (ln1, w_in, q_norm, w_uq, kv_norm, w_ukv, w_pool, pool_scale, attn_out_norm,
              pool_out_norm, w_out, ln2, w_router, b_router, w_gate, b_gate, w_up, b_up,
              w_down, b_down)
    layers = [_prep_layer(l, *params) for l in range(ln1.shape[0])]
    return (_trunk(x_prompt, layers, final_norm), _trunk(x_sample, layers, final_norm))
```
